```python
import jax, jax.numpy as jnp
from jax import lax
import numpy as np

D_MODEL = 2048
BATCH = 4
SEQ = 2048
DEPTH = 2

GRID_W = 64
CTX_LEN = 256
LRU_WIDTH = 1024
LRU_BLOCKS = 8
LRU_BLOCK_W = LRU_WIDTH // LRU_BLOCKS
LRU_C = 8.0
CONV_W = 4
MLA_HEADS = 8
MLA_Q_RANK = 512
MLA_KV_RANK = 256
MLA_NOPE = 128
MLA_ROPE = 64
MLA_V = 128
MLA_SCALE = (MLA_NOPE + MLA_ROPE) ** -0.5
SWA_HEADS = 16
SWA_KV_HEADS = 4
SWA_HEAD_DIM = 64
SWA_REP = SWA_HEADS // SWA_KV_HEADS
SWA_SCALE = SWA_HEAD_DIM ** -0.5
WINDOW = 128
BLOCK = 128
N_BRANCH = 3
BRANCH_W = 1024
MIX_COLS = (LRU_WIDTH, LRU_WIDTH, MLA_Q_RANK, MLA_KV_RANK, MLA_ROPE,
            SWA_HEADS * SWA_HEAD_DIM, SWA_KV_HEADS * SWA_HEAD_DIM, SWA_KV_HEADS * SWA_HEAD_DIM)
MIX_IN = 4416
IN_COLS = MIX_IN + N_BRANCH * D_MODEL
FFN_HIDDEN = -(-8 * D_MODEL // (3 * 256)) * 256
ROPE_BASE = 10000.0
LN_EPS = 1e-5
RMS_EPS = 1e-6

kernel_name = 'hybrid_rglru_mla_swa_deepnorm_dit'

F32 = jnp.float32


def _layer_norm(x, g, b):
    xf = x.astype(F32)
    mu = jnp.mean(xf, -1, keepdims=True)
    var = jnp.mean(jnp.square(xf - mu), -1, keepdims=True)
    return ((xf - mu) * lax.rsqrt(var + LN_EPS) * g.astype(F32) + b.astype(F32)).astype(x.dtype)


def _rms_norm(x, g):
    xf = x.astype(F32)
    return (xf * lax.rsqrt(jnp.mean(xf * xf, -1, keepdims=True) + RMS_EPS) * g.astype(F32)).astype(x.dtype)


def _axial_angles(rows, rot_dim):
    row = jnp.repeat(jnp.arange(rows, dtype=F32), GRID_W)
    col = jnp.tile(jnp.arange(GRID_W, dtype=F32), rows)
    half = rot_dim // 2
    inv = ROPE_BASE ** (-jnp.arange(0, half, 2, dtype=F32) / half)
    return row[:, None] * inv, col[:, None] * inv


def _rope_1d(x, ang):
    f = ang.shape[-1]
    cos = jnp.cos(ang)[None, :, None, :]
    sin = jnp.sin(ang)[None, :, None, :]
    x1 = x[..., :f].astype(F32)
    x2 = x[..., f:].astype(F32)
    return jnp.concatenate([x1 * cos - x2 * sin, x1 * sin + x2 * cos], -1).astype(x.dtype)


def _axial_rope(x, ang_row, ang_col):
    half = x.shape[-1] // 2
    return jnp.concatenate([_rope_1d(x[..., :half], ang_row), _rope_1d(x[..., half:], ang_col)], -1)


def _centred_dwconv(x, w, b):
    t = x.shape[1]
    left = CONV_W // 2
    xp = jnp.pad(x, ((0, 0), (left, CONV_W - 1 - left), (0, 0)))
    out = xp[:, 0:t] * w[0]
    for k in range(1, CONV_W):
        out = out + xp[:, k:k + t] * w[k]
    return out + b


def _blockdiag(x, w, b):
    xb = x.reshape(x.shape[:-1] + (LRU_BLOCKS, LRU_BLOCK_W))
    return (jnp.einsum('btnj,njk->btnk', xb, w) + b).reshape(x.shape)


def _lru_coeffs(x, wr, br, wi, bi, lam):
    r = jax.nn.sigmoid(_blockdiag(x, wr, br).astype(F32))
    i = jax.nn.sigmoid(_blockdiag(x, wi, bi).astype(F32))
    log_a = -LRU_C * r * jax.nn.softplus(-lam.astype(F32))
    a = jnp.exp(log_a)
    mult = jnp.sqrt(-jnp.expm1(2.0 * log_a))
    return a, mult * i * x.astype(F32)


def _lin_combine(left, right):
    a_l, b_l = left
    a_r, b_r = right
    return a_l * a_r, a_r * b_l + b_r


def _linear_scan(a, b, h0):
    b = b.at[:, 0].add(a[:, 0] * h0)
    return lax.associative_scan(_lin_combine, (a, b), axis=1)[1]


def _rglru(xl, xc, lp):
    xl = _centred_dwconv(xl, lp['conv_w'], lp['conv_b'])
    xc = _centred_dwconv(xc, lp['conv_w'], lp['conv_b'])
    h0 = jnp.zeros((xl.shape[0], xl.shape[2]), F32)
    rev = lambda t: jnp.flip(t, axis=1)
    hs_l, hs_c = [], []
    for d in range(2):
        prm = (lp['lru_wr'][d], lp['lru_br'][d], lp['lru_wi'][d], lp['lru_bi'][d], lp['lru_lambda'][d])
        a_c, b_c = _lru_coeffs(xc, *prm)
        a_l, b_l = _lru_coeffs(xl, *prm)
        if d == 1:
            a_c, b_c, a_l, b_l = rev(a_c), rev(b_c), rev(a_l), rev(b_l)
        s_c = _linear_scan(a_c, b_c, h0)
        s_l = _linear_scan(a_l, b_l, s_c[:, -1])
        if d == 1:
            s_c, s_l = rev(s_c), rev(s_l)
        hs_l.append(s_l)
        hs_c.append(s_c)
    return hs_l[0] + hs_l[1], hs_c[0] + hs_c[1]


def _mla_q(q_lin, lp, ang):
    bsz, t, _ = q_lin.shape
    q = (_rms_norm(q_lin, lp['mla_q_norm']) @ lp['mla_w_q_up']).reshape(bsz, t, MLA_HEADS, MLA_NOPE + MLA_ROPE)
    q_nope, q_rope = q[..., :MLA_NOPE], q[..., MLA_NOPE:]
    if ang is not None:
        q_rope = _axial_rope(q_rope, *ang)
    return jnp.concatenate([q_nope, q_rope], -1)


def _mla_kv(kv_lin, kr_lin, lp, ang):
    bsz, t, _ = kv_lin.shape
    kv = (_rms_norm(kv_lin, lp['mla_kv_norm']) @ lp['mla_w_kv_up']).reshape(bsz, t, MLA_HEADS, MLA_NOPE + MLA_V)
    k_nope, v = kv[..., :MLA_NOPE], kv[..., MLA_NOPE:]
    k_rope = kr_lin[:, :, None, :]
    if ang is not None:
        k_rope = _axial_rope(k_rope, *ang)
    k = jnp.concatenate([k_nope, jnp.broadcast_to(k_rope, (bsz, t, MLA_HEADS, MLA_ROPE))], -1)
    return k, v


def _softmax_attend(q, k, v, scale):
    s = jnp.einsum('bqhd,bkhd->bhqk', q, k).astype(F32) * scale
    p = jax.nn.softmax(s, axis=-1).astype(v.dtype)
    return jnp.einsum('bhqk,bkhd->bqhd', p, v)


def _blocked_attend(q, k, v, scale):
    bsz, t, h, dk = q.shape
    nb = t // BLOCK
    qb = jnp.moveaxis(q.reshape(bsz, nb, BLOCK, h, dk), 1, 0)
    out = lax.map(lambda qi: _softmax_attend(qi, k, v, scale), qb)
    return jnp.moveaxis(out, 0, 1).reshape(bsz, t, h * v.shape[-1])


def _swa_latent(q, k, v, kc, vc, sinks):
    bsz, s, _, d = q.shape
    g, r, nb = SWA_KV_HEADS, SWA_REP, s // BLOCK
    qb = q.reshape(bsz, nb, BLOCK, g, r, d)

    def band(t):
        tp = jnp.pad(t, ((0, 0), (BLOCK, BLOCK), (0, 0), (0, 0))).reshape(bsz, nb + 2, BLOCK, g, d)
        return jnp.concatenate([tp[:, :-2], tp[:, 1:-1], tp[:, 2:]], axis=2)

    kb, vb = band(k), band(v)
    s_band = jnp.einsum('bnqgrd,bnkgd->bgrnqk', qb, kb).astype(F32) * SWA_SCALE
    qpos = jnp.arange(nb)[:, None, None] * BLOCK + jnp.arange(BLOCK)[None, :, None]
    kpos = jnp.arange(nb)[:, None, None] * BLOCK - BLOCK + jnp.arange(3 * BLOCK)[None, None, :]
    valid = (jnp.abs(kpos - qpos) <= WINDOW) & (kpos >= 0) & (kpos < s)
    s_band = jnp.where(valid, s_band, -jnp.inf)
    s_ctx = jnp.einsum('bnqgrd,bcgd->bgrnqc', qb, kc).astype(F32) * SWA_SCALE
    sink = jnp.broadcast_to(sinks.astype(F32).reshape(g, r)[None, :, :, None, None, None], s_ctx.shape[:-1] + (1,))
    p = jax.nn.softmax(jnp.concatenate([sink, s_ctx, s_band], -1), axis=-1)
    n_ctx = kc.shape[1]
    p_ctx = p[..., 1:1 + n_ctx].astype(v.dtype)
    p_band = p[..., 1 + n_ctx:].astype(v.dtype)
    out = (jnp.einsum('bgrnqc,bcgd->bnqgrd', p_ctx, vc)
           + jnp.einsum('bgrnqk,bnkgd->bnqgrd', p_band, vb))
    return out.reshape(bsz, s, SWA_HEADS * d)


def _swa_context(qc, kc, vc, sinks):
    bsz, n_ctx, _, d = qc.shape
    g, r = SWA_KV_HEADS, SWA_REP
    qg = qc.reshape(bsz, n_ctx, g, r, d)
    s = jnp.einsum('bqgrd,bkgd->bgrqk', qg, kc).astype(F32) * SWA_SCALE
    sink = jnp.broadcast_to(sinks.astype(F32).reshape(g, r)[None, :, :, None, None], s.shape[:-1] + (1,))
    p = jax.nn.softmax(jnp.concatenate([sink, s], -1), axis=-1)[..., 1:].astype(vc.dtype)
    return jnp.einsum('bgrqk,bkgd->bqgrd', p, vc).reshape(bsz, n_ctx, SWA_HEADS * d)


def _merge(gate_lin, ys, w_branch, w_out):
    yst = jnp.stack(ys, axis=2)
    z = jnp.einsum('btnw,nwd->btnd', yst, w_branch)
    gates = jax.nn.sigmoid(gate_lin.reshape(z.shape).astype(F32)).astype(z.dtype)
    return jnp.sum(gates * z, axis=2) @ w_out


def _mixer(u, uc, ang_mla, ang_swa, lp, need_ctx):
    bsz, s, _ = u.shape
    n_ctx = uc.shape[1]
    g, d = SWA_KV_HEADS, SWA_HEAD_DIM
    split_at = np.cumsum(MIX_COLS)[:-1].tolist()
    proj = u @ lp['w_in']
    projc = uc @ lp['w_in'][:, :MIX_IN]
    a_x, a_g, b_q, b_kv, b_kr, c_q, c_k, c_v = jnp.split(proj[..., :MIX_IN], split_at, axis=-1)
    a_xc, a_gc, b_qc, b_kvc, b_krc, c_qc, c_kc, c_vc = jnp.split(projc, split_at, axis=-1)

    h_l, h_c = _rglru(a_x, a_xc, lp)
    y_a = (h_l * jax.nn.gelu(a_g.astype(F32))).astype(u.dtype)

    k_b, v_b = _mla_kv(b_kv, b_kr, lp, ang_mla)
    kc_b, vc_b = _mla_kv(b_kvc, b_krc, lp, None)
    q_b = _mla_q(b_q, lp, ang_mla)
    y_b = _blocked_attend(q_b, jnp.concatenate([kc_b, k_b], 1), jnp.concatenate([vc_b, v_b], 1), MLA_SCALE)

    q_c = _axial_rope(c_q.reshape(bsz, s, SWA_HEADS, d), *ang_swa)
    k_c = _axial_rope(c_k.reshape(bsz, s, g, d), *ang_swa)
    v_c = c_v.reshape(bsz, s, g, d)
    kc_c = c_kc.reshape(bsz, n_ctx, g, d)
    vc_c = c_vc.reshape(bsz, n_ctx, g, d)
    y_c = _swa_latent(q_c, k_c, v_c, kc_c, vc_c, lp['swa_sinks'])

    y = _merge(proj[..., MIX_IN:], (y_a, y_b.astype(u.dtype), y_c.astype(u.dtype)), lp['w_branch'], lp['w_out'])
    if not need_ctx:
        return y, None
    yc_a = (h_c * jax.nn.gelu(a_gc.astype(F32))).astype(uc.dtype)
    qc_b = _mla_q(b_qc, lp, None)
    yc_b = _softmax_attend(qc_b, kc_b, vc_b, MLA_SCALE).reshape(bsz, n_ctx, MLA_HEADS * MLA_V)
    yc_c = _swa_context(c_qc.reshape(bsz, n_ctx, SWA_HEADS, d), kc_c, vc_c, lp['swa_sinks'])
    yc = _merge(uc @ lp['w_in'][:, MIX_IN:], (yc_a, yc_b.astype(uc.dtype), yc_c.astype(uc.dtype)),
                lp['w_branch'], lp['w_out'])
    return y, yc


def _swiglu(u, w_in, w_out):
    gt, up = jnp.split(u @ w_in, 2, axis=-1)
    return (jax.nn.silu(gt) * up) @ w_out


def setup_inputs(seed: int = 0) -> dict:
    key = jax.random.key(seed)
    ks = iter(jax.random.split(key, 32))

    def nrm(shape, scale):
        return jax.random.normal(next(ks), shape, F32) * scale

    beta = (8 * DEPTH) ** -0.25
    a_c = jax.random.uniform(next(ks), (DEPTH, 2, LRU_WIDTH), F32, minval=0.9, maxval=0.999)
    sg = a_c ** (1.0 / LRU_C)
    lam = jnp.log(sg) - jnp.log1p(-sg)
    return {
        'x': nrm((BATCH, SEQ, D_MODEL), 1.0),
        'c': nrm((BATCH, D_MODEL), 1.0),
        'ctx': nrm((BATCH, CTX_LEN, D_MODEL), 1.0),
        'c_ctx': nrm((D_MODEL,), 1.0),
        'w_ada': nrm((DEPTH, D_MODEL, 6 * D_MODEL), 0.5 * D_MODEL ** -0.5),
        'b_ada': nrm((DEPTH, 6 * D_MODEL), 0.01),
        'w_in': nrm((DEPTH, D_MODEL, IN_COLS), D_MODEL ** -0.5),
        'conv_w': nrm((DEPTH, CONV_W, LRU_WIDTH), CONV_W ** -0.5),
        'conv_b': nrm((DEPTH, LRU_WIDTH), 0.01),
        'lru_wr': nrm((DEPTH, 2, LRU_BLOCKS, LRU_BLOCK_W, LRU_BLOCK_W), LRU_BLOCK_W ** -0.5),
        'lru_br': nrm((DEPTH, 2, LRU_BLOCKS, LRU_BLOCK_W), 0.01),
        'lru_wi': nrm((DEPTH, 2, LRU_BLOCKS, LRU_BLOCK_W, LRU_BLOCK_W), LRU_BLOCK_W ** -0.5),
        'lru_bi': nrm((DEPTH, 2, LRU_BLOCKS, LRU_BLOCK_W), 0.01),
        'lru_lambda': lam,
        'mla_q_norm': 1.0 + nrm((DEPTH, MLA_Q_RANK), 0.02),
        'mla_w_q_up': nrm((DEPTH, MLA_Q_RANK, MLA_HEADS * (MLA_NOPE + MLA_ROPE)), MLA_Q_RANK ** -0.5),
        'mla_kv_norm': 1.0 + nrm((DEPTH, MLA_KV_RANK), 0.02),
        'mla_w_kv_up': nrm((DEPTH, MLA_KV_RANK, MLA_HEADS * (MLA_NOPE + MLA_V)), MLA_KV_RANK ** -0.5),
        'swa_sinks': nrm((DEPTH, SWA_HEADS), 0.5),
        'w_branch': nrm((DEPTH, N_BRANCH, BRANCH_W, D_MODEL), BRANCH_W ** -0.5),
        'w_out': nrm((DEPTH, D_MODEL, D_MODEL), beta * D_MODEL ** -0.5),
        'ln1_g': 1.0 + nrm((DEPTH, D_MODEL), 0.02),
        'ln1_b': nrm((DEPTH, D_MODEL), 0.02),
        'w_ffn_in': nrm((DEPTH, D_MODEL, 2 * FFN_HIDDEN), D_MODEL ** -0.5),
        'w_ffn_out': nrm((DEPTH, FFN_HIDDEN, D_MODEL), beta * FFN_HIDDEN ** -0.5),
        'ln2_g': 1.0 + nrm((DEPTH, D_MODEL), 0.02),
        'ln2_b': nrm((DEPTH, D_MODEL), 0.02),
    }


def reference(x, c, ctx, c_ctx, w_ada, b_ada, w_in, conv_w, conv_b, lru_wr, lru_br, lru_wi, lru_bi,
              lru_lambda, mla_q_norm, mla_w_q_up, mla_kv_norm, mla_w_kv_up, swa_sinks, w_branch, w_out,
              ln1_g, ln1_b, w_ffn_in, w_ffn_out, ln2_g, ln2_b):
    alpha = (2 * DEPTH) ** 0.25
    rows = x.shape[1] // GRID_W
    ang_mla = _axial_angles(rows, MLA_ROPE)
    ang_swa = _axial_angles(rows, SWA_HEAD_DIM)
    c_act = jax.nn.silu(c)
    cc_act = jax.nn.silu(c_ctx)
    for l in range(DEPTH):
        need_ctx = l < DEPTH - 1
        lp = {
            'w_in': w_in[l], 'conv_w': conv_w[l], 'conv_b': conv_b[l],
            'lru_wr': lru_wr[l], 'lru_br': lru_br[l], 'lru_wi': lru_wi[l], 'lru_bi': lru_bi[l],
            'lru_lambda': lru_lambda[l], 'mla_q_norm': mla_q_norm[l], 'mla_w_q_up': mla_w_q_up[l],
            'mla_kv_norm': mla_kv_norm[l], 'mla_w_kv_up': mla_w_kv_up[l], 'swa_sinks': swa_sinks[l],
            'w_branch': w_branch[l], 'w_out': w_out[l],
        }
        ada = c_act @ w_ada[l] + b_ada[l]
        adac = cc_act @ w_ada[l] + b_ada[l]
        sh1, sc1, g1, sh2, sc2, g2 = jnp.split(ada[:, None, :], 6, axis=-1)
        sh1c, sc1c, g1c, sh2c, sc2c, g2c = jnp.split(adac, 6, axis=-1)

        u = x * (1.0 + sc1) + sh1
        uc = ctx * (1.0 + sc1c) + sh1c
        y, yc = _mixer(u, uc, ang_mla, ang_swa, lp, need_ctx)
        x = _layer_norm(alpha * x + g1 * y, ln1_g[l], ln1_b[l])
        x = _layer_norm(alpha * x + g2 * _swiglu(x * (1.0 + sc2) + sh2, w_ffn_in[l], w_ffn_out[l]),
                        ln2_g[l], ln2_b[l])
        if need_ctx:
            ctx = _layer_norm(alpha * ctx + g1c * yc, ln1_g[l], ln1_b[l])
            ctx = _layer_norm(alpha * ctx + g2c * _swiglu(ctx * (1.0 + sc2c) + sh2c, w_ffn_in[l], w_ffn_out[l]),
                              ln2_g[l], ln2_b[l])
    return x
```

```python
import functools
import math

import jax
import jax.numpy as jnp
from jax import lax
from jax.experimental import pallas as pl
from jax.experimental.pallas import tpu as pltpu

F32 = jnp.float32
BF16 = jnp.bfloat16

D_MODEL = 2048
BATCH = 4
SEQ = 2048
DEPTH = 2
GRID_W = 64
CTX_LEN = 256
LRU_WIDTH = 1024
LRU_BLOCKS = 8
LRU_BLOCK_W = LRU_WIDTH // LRU_BLOCKS
LRU_C = 8.0
CONV_W = 4
MLA_HEADS = 8
MLA_Q_RANK = 512
MLA_KV_RANK = 256
MLA_NOPE = 128
MLA_ROPE = 64
MLA_V = 128
MLA_SCALE = (MLA_NOPE + MLA_ROPE) ** -0.5
SWA_HEADS = 16
SWA_KV_HEADS = 4
SWA_HEAD_DIM = 64
SWA_REP = SWA_HEADS // SWA_KV_HEADS
SWA_SCALE = SWA_HEAD_DIM ** -0.5
WINDOW = 128
N_BRANCH = 3
BRANCH_W = 1024
MIX_IN = 4416
FFN_HIDDEN = -(-8 * D_MODEL // (3 * 256)) * 256
ROPE_BASE = 10000.0
LN_EPS = 1e-5
RMS_EPS = 1e-6
ALPHA = (2 * DEPTH) ** 0.25

T_TOK = CTX_LEN + SEQ
ROWS = BATCH * T_TOK
LANES = 128
SUBLANES = 8
MLA_SLOT = 2 * LANES

P_AX, P_AG, P_BQ, P_BKV, P_BKR = 0, 1024, 2048, 2560, 2816
P_CQ, P_CK, P_CV, P_GATE = 3072, 4096, 4352, 4608
P_COLS = P_GATE + N_BRANCH * D_MODEL

VMEM_LIMIT = 56 * 1024 * 1024

TM_MIX = 1152
TN_MIX = 768
TM_MERGE = 576
TJ_MERGE = 512
TM_FFN = 768
TJ_FFN = 512
TM_PREP = 576
TQ_ATT = 256
TN_ADA = 1024


def _params(sem):
    return pltpu.CompilerParams(dimension_semantics=sem, vmem_limit_bytes=VMEM_LIMIT)


def _row_select(i, tiles_per_batch, tm, ctx_vec, batch_vec):
    rows = lax.broadcasted_iota(jnp.int32, (tm, 1), 0)
    n_ctx = jnp.where(i % tiles_per_batch == 0, CTX_LEN, 0)
    return jnp.where(rows < n_ctx, ctx_vec, batch_vec)


def _layer_norm(v, g, b):
    mu = jnp.mean(v, -1, keepdims=True)
    d = v - mu
    var = jnp.mean(d * d, -1, keepdims=True)
    return d * lax.rsqrt(var + LN_EPS) * g + b


def _rms_norm(v, g):
    return v * lax.rsqrt(jnp.mean(v * v, -1, keepdims=True) + RMS_EPS) * g


def _swap_pairs(x):
    lane = lax.broadcasted_iota(jnp.int32, (1, LANES), 1)
    first = (lane & 16) == 0
    return jnp.where(first, pltpu.roll(x, LANES - 16, 1), pltpu.roll(x, 16, 1))


def _rope(x, cos, sin):
    return x * cos + _swap_pairs(x) * sin


def _ada_kernel(c_ref, w_ref, b_ref, o_ref, *, chunks_per_vec):
    j = pl.program_id(1)
    c = c_ref[...]
    a = (c * jax.nn.sigmoid(c)).astype(BF16)
    acc = jnp.dot(a, w_ref[0].astype(BF16), preferred_element_type=F32)
    vec = j // chunks_per_vec
    one = jnp.where((vec == 1) | (vec == 4), 1.0, 0.0)
    o_ref[0, 0] = acc + b_ref[0] + one


def _ada(cvec, w_ada, b_ada):
    tn = TN_ADA
    cpv = D_MODEL // tn
    return pl.pallas_call(
        functools.partial(_ada_kernel, chunks_per_vec=cpv),
        grid=(DEPTH, 6 * cpv),
        in_specs=[
            pl.BlockSpec((SUBLANES, D_MODEL), lambda l, j: (0, 0)),
            pl.BlockSpec((1, D_MODEL, tn), lambda l, j: (l, 0, j)),
            pl.BlockSpec((1, 1, tn), lambda l, j: (l, 0, j)),
        ],
        out_specs=pl.BlockSpec((1, 1, SUBLANES, tn), lambda l, j: (l, j // cpv, 0, j % cpv)),
        out_shape=jax.ShapeDtypeStruct((DEPTH, 6, SUBLANES, D_MODEL), F32),
        compiler_params=_params(("arbitrary", "arbitrary")),
        name="ada",
    )(cvec, w_ada, b_ada.reshape(DEPTH, 1, 6 * D_MODEL))


def _vec_specs(vec, tiles_per_batch):
    return [
        pl.BlockSpec((1, 1, 1, D_MODEL), lambda i, j: (vec, i // tiles_per_batch, 0, 0)),
        pl.BlockSpec((1, 1, 1, D_MODEL), lambda i, j: (vec, BATCH, 0, 0)),
    ]


def _mixin_kernel(x_ref, scb_ref, scc_ref, shb_ref, shc_ref, w_ref, o_ref, u_ref, *, tpb, tm):
    i = pl.program_id(0)

    @pl.when(pl.program_id(1) == 0)
    def _():
        sc = _row_select(i, tpb, tm, scc_ref[0, 0], scb_ref[0, 0])
        sh = _row_select(i, tpb, tm, shc_ref[0, 0], shb_ref[0, 0])
        u_ref[...] = (x_ref[...] * sc + sh).astype(BF16)

    o_ref[...] = jnp.dot(u_ref[...], w_ref[...], preferred_element_type=F32).astype(o_ref.dtype)


def _mixin(x, ada_l, w_p):
    tm, tn = TM_MIX, TN_MIX
    tpb = T_TOK // tm
    return pl.pallas_call(
        functools.partial(_mixin_kernel, tpb=tpb, tm=tm),
        grid=(ROWS // tm, P_COLS // tn),
        in_specs=[pl.BlockSpec((tm, D_MODEL), lambda i, j: (i, 0))]
        + _vec_specs(1, tpb)
        + _vec_specs(0, tpb)
        + [pl.BlockSpec((D_MODEL, tn), lambda i, j: (0, j))],
        out_specs=pl.BlockSpec((tm, tn), lambda i, j: (i, j)),
        out_shape=jax.ShapeDtypeStruct((ROWS, P_COLS), BF16),
        scratch_shapes=[pltpu.VMEM((tm, D_MODEL), BF16)],
        compiler_params=_params(("arbitrary", "arbitrary")),
        name="mix_in",
    )(x, ada_l, ada_l, ada_l, ada_l, w_p)


def _lru_kernel(ax_ref, ag_ref, cw_ref, cb_ref, wr_ref, br_ref, wi_ref, bi_ref, lam_ref, y_ref,
                a_s, b_s, h_s):
    n_tok = T_TOK
    x = ax_ref[...].astype(F32)
    t = lax.broadcasted_iota(jnp.int32, (n_tok, 1), 0)
    is_ctx = t < CTX_LEN
    t_loc = jnp.where(is_ctx, t, t - CTX_LEN)
    seg_len = jnp.where(is_ctx, CTX_LEN, SEQ)
    xm1 = jnp.where(t_loc >= 1, pltpu.roll(x, 1, 0), 0.0)
    xm2 = jnp.where(t_loc >= 2, pltpu.roll(x, 2, 0), 0.0)
    xp1 = jnp.where(t_loc <= seg_len - 2, pltpu.roll(x, n_tok - 1, 0), 0.0)
    w = cw_ref[...]
    xc = xm2 * w[0:1] + xm1 * w[1:2] + x * w[2:3] + xp1 * w[3:4] + cb_ref[...]
    xb = xc.astype(BF16)
    sub = t & (SUBLANES - 1)

    for d in range(2):
        r = jax.nn.sigmoid(jnp.dot(xb, wr_ref[d, 0], preferred_element_type=F32) + br_ref[d])
        ig = jax.nn.sigmoid(jnp.dot(xb, wi_ref[d, 0], preferred_element_type=F32) + bi_ref[d])
        z = -lam_ref[d]
        softplus = jnp.maximum(z, 0.0) + jnp.log1p(jnp.exp(-jnp.abs(z)))
        log_a = (-LRU_C) * r * softplus
        a = jnp.exp(log_a)
        bb = jnp.sqrt(jnp.tanh(-log_a) * (a * a + 1.0)) * ig * xc
        for s in (1, 2, 4):
            if d == 0:
                m = sub >= s
                shift = s
            else:
                m = sub < SUBLANES - s
                shift = n_tok - s
            a_sh = pltpu.roll(a, shift, 0)
            b_sh = pltpu.roll(bb, shift, 0)
            bb = jnp.where(m, bb + a * b_sh, bb)
            a = jnp.where(m, a * a_sh, a)
        a_s[d] = a
        b_s[d] = bb

    zero = jnp.zeros((1, LANES), F32)

    def fwd(g, h):
        rows = pl.ds(pl.multiple_of(g * SUBLANES, SUBLANES), SUBLANES)
        hg = b_s[0, rows, :] + a_s[0, rows, :] * h
        h_s[rows, :] = hg
        return hg[SUBLANES - 1:SUBLANES]

    lax.fori_loop(0, n_tok // SUBLANES, fwd, zero, unroll=4)

    def bwd(k, h, *, last):
        g = last - k
        rows = pl.ds(pl.multiple_of(g * SUBLANES, SUBLANES), SUBLANES)
        hg = b_s[1, rows, :] + a_s[1, rows, :] * h
        h_s[rows, :] += hg
        return hg[0:1]

    g_ctx = CTX_LEN // SUBLANES
    h_c = lax.fori_loop(0, g_ctx, functools.partial(bwd, last=g_ctx - 1), zero, unroll=4)
    lax.fori_loop(0, SEQ // SUBLANES, functools.partial(bwd, last=n_tok // SUBLANES - 1), h_c, unroll=4)

    g = ag_ref[...].astype(F32)
    cdf = 0.5 * (1.0 + jnp.tanh(math.sqrt(2.0 / math.pi) * (g + 0.044715 * (g * g * g))))
    y_ref[...] = (h_s[...] * (g * cdf)).astype(y_ref.dtype)


def _lru(p, conv_w, conv_b, wr, br, wi, bi, lam):
    nb = LRU_BLOCKS
    col = lambda off: (lambda b, n: (b, off // LANES + n))
    vec2 = pl.BlockSpec((2, 1, LANES), lambda b, n: (0, 0, n))
    mat = pl.BlockSpec((2, 1, LRU_BLOCK_W, LRU_BLOCK_W), lambda b, n: (0, n, 0, 0))
    return pl.pallas_call(
        _lru_kernel,
        grid=(BATCH, nb),
        in_specs=[
            pl.BlockSpec((T_TOK, LANES), col(P_AX)),
            pl.BlockSpec((T_TOK, LANES), col(P_AG)),
            pl.BlockSpec((CONV_W, LANES), lambda b, n: (0, n)),
            pl.BlockSpec((1, LANES), lambda b, n: (0, n)),
            mat, vec2, mat, vec2, vec2,
        ],
        out_specs=pl.BlockSpec((T_TOK, LANES), lambda b, n: (b, n)),
        out_shape=jax.ShapeDtypeStruct((ROWS, LRU_WIDTH), BF16),
        scratch_shapes=[
            pltpu.VMEM((2, T_TOK, LANES), F32),
            pltpu.VMEM((2, T_TOK, LANES), F32),
            pltpu.VMEM((T_TOK, LANES), F32),
        ],
        compiler_params=_params(("arbitrary", "arbitrary")),
        name="rg_lru",
    )(p, p, conv_w, conv_b.reshape(1, LRU_WIDTH), wr, br.reshape(2, 1, LRU_WIDTH), wi,
      bi.reshape(2, 1, LRU_WIDTH), lam.reshape(2, 1, LRU_WIDTH))


def _mla_prep_kernel(bq_ref, bkv_ref, bkr_ref, qn_ref, wq_ref, kvn_ref, wk_ref, wv_ref, cos_ref, sin_ref,
                     q_out, k_out, v_out):
    cos = cos_ref[...]
    sin = sin_ref[...]
    qn = _rms_norm(bq_ref[...].astype(F32), qn_ref[...])
    q = jnp.dot(qn.astype(BF16), wq_ref[...], preferred_element_type=F32) * MLA_SCALE
    kvn = _rms_norm(bkv_ref[...].astype(F32), kvn_ref[...]).astype(BF16)
    kn = jnp.dot(kvn, wk_ref[...], preferred_element_type=F32).astype(BF16)
    v_out[...] = jnp.dot(kvn, wv_ref[...], preferred_element_type=F32).astype(BF16)
    kr = _rope(bkr_ref[...].astype(F32), cos, sin).astype(BF16)
    for h in range(MLA_HEADS):
        lo = MLA_SLOT * h
        q_out[:, lo:lo + LANES] = q[:, lo:lo + LANES].astype(BF16)
        q_out[:, lo + LANES:lo + MLA_SLOT] = _rope(q[:, lo + LANES:lo + MLA_SLOT], cos, sin).astype(BF16)
        k_out[:, lo:lo + LANES] = kn[:, LANES * h:LANES * (h + 1)]
        k_out[:, lo + LANES:lo + MLA_SLOT] = kr


def _mla_prep(p, q_norm, wq_p, kv_norm, wk_p, wv_p, cos, sin):
    tm = TM_PREP
    tpb = T_TOK // tm
    full = lambda shape: pl.BlockSpec(shape, lambda i: (0,) * len(shape))
    return pl.pallas_call(
        _mla_prep_kernel,
        grid=(ROWS // tm,),
        in_specs=[
            pl.BlockSpec((tm, MLA_Q_RANK), lambda i: (i, P_BQ // MLA_Q_RANK)),
            pl.BlockSpec((tm, MLA_KV_RANK), lambda i: (i, P_BKV // MLA_KV_RANK)),
            pl.BlockSpec((tm, LANES), lambda i: (i, P_BKR // LANES)),
            full((1, MLA_Q_RANK)),
            full((MLA_Q_RANK, MLA_HEADS * MLA_SLOT)),
            full((1, MLA_KV_RANK)),
            full((MLA_KV_RANK, MLA_HEADS * MLA_NOPE)),
            full((MLA_KV_RANK, MLA_HEADS * MLA_V)),
            pl.BlockSpec((tm, LANES), lambda i: (i % tpb, 0)),
            pl.BlockSpec((tm, LANES), lambda i: (i % tpb, 0)),
        ],
        out_specs=[
            pl.BlockSpec((tm, MLA_HEADS * MLA_SLOT), lambda i: (i, 0)),
            pl.BlockSpec((tm, MLA_HEADS * MLA_SLOT), lambda i: (i, 0)),
            pl.BlockSpec((tm, MLA_HEADS * MLA_V), lambda i: (i, 0)),
        ],
        out_shape=[
            jax.ShapeDtypeStruct((ROWS, MLA_HEADS * MLA_SLOT), BF16),
            jax.ShapeDtypeStruct((ROWS, MLA_HEADS * MLA_SLOT), BF16),
            jax.ShapeDtypeStruct((ROWS, MLA_HEADS * MLA_V), BF16),
        ],
        compiler_params=_params(("arbitrary",)),
        name="mla_prep",
    )(p, p, p, q_norm.reshape(1, -1), wq_p, kv_norm.reshape(1, -1), wk_p, wv_p, cos, sin)


def _mla_attn_kernel(q_ref, k_ref, v_ref, o_ref):
    def attend(n_keys):
        for h in range(MLA_HEADS):
            q = q_ref[:, MLA_SLOT * h:MLA_SLOT * (h + 1)]
            k = k_ref[0:n_keys, MLA_SLOT * h:MLA_SLOT * (h + 1)]
            s = lax.dot_general(q, k, (((1,), (1,)), ((), ())), preferred_element_type=F32)
            m = jnp.max(s, -1, keepdims=True)
            p = jnp.exp(s - m)
            l = jnp.sum(p, -1, keepdims=True)
            o = jnp.dot(p.astype(BF16), v_ref[0:n_keys, MLA_V * h:MLA_V * (h + 1)], preferred_element_type=F32)
            o_ref[:, MLA_V * h:MLA_V * (h + 1)] = (o / l).astype(o_ref.dtype)

    @pl.when(pl.program_id(1) == 0)
    def _():
        attend(CTX_LEN)

    @pl.when(pl.program_id(1) > 0)
    def _():
        attend(T_TOK)


def _mla_attn(q, k, v):
    tq = TQ_ATT
    assert tq == CTX_LEN
    tpb = T_TOK // tq
    return pl.pallas_call(
        _mla_attn_kernel,
        grid=(BATCH, tpb),
        in_specs=[
            pl.BlockSpec((tq, MLA_HEADS * MLA_SLOT), lambda b, i: (b * tpb + i, 0)),
            pl.BlockSpec((T_TOK, MLA_HEADS * MLA_SLOT), lambda b, i: (b, 0)),
            pl.BlockSpec((T_TOK, MLA_HEADS * MLA_V), lambda b, i: (b, 0)),
        ],
        out_specs=pl.BlockSpec((tq, MLA_HEADS * MLA_V), lambda b, i: (b * tpb + i, 0)),
        out_shape=jax.ShapeDtypeStruct((ROWS, MLA_HEADS * MLA_V), BF16),
        compiler_params=_params(("arbitrary", "arbitrary")),
        name="mla_attn",
    )(q, k, v)


def _dup_half(x, upper):
    lane = lax.broadcasted_iota(jnp.int32, (1, LANES), 1)
    r = pltpu.roll(x, LANES // 2, 1)
    out = jnp.where(lane < LANES // 2, r, x) if upper else jnp.where(lane < LANES // 2, x, r)
    return out.astype(BF16)


def _swa_kernel(q_ref, k_ref, v_ref, cos_ref, sin_ref, sink_ref, o_ref):
    tq = TQ_ATT
    band = tq + 2 * WINDOW
    qt = pl.program_id(1)
    q_row = pl.multiple_of(qt * tq, tq)
    lane = lax.broadcasted_iota(jnp.int32, (1, LANES), 1)
    low = lane < LANES // 2
    cos_q = cos_ref[pl.ds(q_row, tq), :]
    sin_q = sin_ref[pl.ds(q_row, tq), :]

    def q_block(c):
        x = q_ref[:, LANES * c:LANES * (c + 1)].astype(F32)
        return _rope(x, cos_q, sin_q) * SWA_SCALE

    def softmax_out(qm, sink, keys, vals, masks):
        ss = []
        for kk, mk in zip(keys, masks):
            s = lax.dot_general(qm, kk, (((1,), (1,)), ((), ())), preferred_element_type=F32)
            ss.append(s if mk is None else jnp.where(mk, s, -jnp.inf))
        m = jnp.maximum(functools.reduce(jnp.maximum, [jnp.max(s, -1, keepdims=True) for s in ss]), sink)
        ps = [jnp.exp(s - m) for s in ss]
        l = functools.reduce(jnp.add, [jnp.sum(p, -1, keepdims=True) for p in ps]) + jnp.exp(sink - m)
        o = functools.reduce(jnp.add, [jnp.dot(p.astype(BF16), vv, preferred_element_type=F32)
                                       for p, vv in zip(ps, vals)])
        return o / l

    def run(keys_of, vals_of, masks):
        for g in range(SWA_KV_HEADS):
            keys = keys_of(g)
            vals = vals_of(g)
            for c in (2 * g, 2 * g + 1):
                qc = q_block(c)
                o_lo = softmax_out(jnp.where(low, qc, 0.0).astype(BF16), sink_ref[2 * c], keys, vals, masks)
                o_hi = softmax_out(jnp.where(low, 0.0, qc).astype(BF16), sink_ref[2 * c + 1], keys, vals, masks)
                o_ref[:, LANES * c:LANES * (c + 1)] = jnp.where(low, o_lo, o_hi).astype(o_ref.dtype)

    def ctx_kv(ref, g):
        return _dup_half(ref[0:CTX_LEN, LANES * (g // 2):LANES * (g // 2 + 1)].astype(F32), g % 2 == 1)

    @pl.when(qt == 0)
    def _():
        run(lambda g: [ctx_kv(k_ref, g)], lambda g: [ctx_kv(v_ref, g)], [None])

    @pl.when(qt > 0)
    def _():
        start = pl.multiple_of(jnp.minimum(q_row - WINDOW, T_TOK - band), LANES)
        qpos = q_row - CTX_LEN + lax.broadcasted_iota(jnp.int32, (tq, 1), 0)
        kpos = start - CTX_LEN + lax.broadcasted_iota(jnp.int32, (1, band), 1)
        valid = (jnp.abs(kpos - qpos) <= WINDOW) & (kpos >= 0)
        cos_k = cos_ref[pl.ds(start, band), :]
        sin_k = sin_ref[pl.ds(start, band), :]

        def band_k(g):
            kb = k_ref[pl.ds(start, band), LANES * (g // 2):LANES * (g // 2 + 1)].astype(F32)
            return _dup_half(_rope(kb, cos_k, sin_k), g % 2 == 1)

        def band_v(g):
            return _dup_half(v_ref[pl.ds(start, band), LANES * (g // 2):LANES * (g // 2 + 1)].astype(F32), g % 2 == 1)

        run(lambda g: [ctx_kv(k_ref, g), band_k(g)], lambda g: [ctx_kv(v_ref, g), band_v(g)], [None, valid])


def _swa(p, cos, sin, sinks):
    tq = TQ_ATT
    tpb = T_TOK // tq
    kvw = SWA_KV_HEADS * SWA_HEAD_DIM
    return pl.pallas_call(
        _swa_kernel,
        grid=(BATCH, tpb),
        in_specs=[
            pl.BlockSpec((tq, SWA_HEADS * SWA_HEAD_DIM), lambda b, i: (b * tpb + i, P_CQ // (SWA_HEADS * SWA_HEAD_DIM))),
            pl.BlockSpec((T_TOK, kvw), lambda b, i: (b, P_CK // kvw)),
            pl.BlockSpec((T_TOK, kvw), lambda b, i: (b, P_CV // kvw)),
            pl.BlockSpec((T_TOK, LANES), lambda b, i: (0, 0)),
            pl.BlockSpec((T_TOK, LANES), lambda b, i: (0, 0)),
            pl.BlockSpec(memory_space=pltpu.SMEM),
        ],
        out_specs=pl.BlockSpec((tq, SWA_HEADS * SWA_HEAD_DIM), lambda b, i: (b * tpb + i, 0)),
        out_shape=jax.ShapeDtypeStruct((ROWS, SWA_HEADS * SWA_HEAD_DIM), BF16),
        compiler_params=_params(("arbitrary", "arbitrary")),
        name="swa",
    )(p, p, p, cos, sin, sinks)


def _merge_kernel(ya_ref, yb_ref, yc_ref, ga_ref, gb_ref, gc_ref, wb_ref, wo_ref, x_ref, gvb_ref, gvc_ref,
                  lng_ref, lnb_ref, o_ref, *, tpb, tm, nj):
    i = pl.program_id(0)
    j = pl.program_id(1)
    z = None
    for n, (y_ref, g_ref) in enumerate(((ya_ref, ga_ref), (yb_ref, gb_ref), (yc_ref, gc_ref))):
        zn = jnp.dot(y_ref[...], wb_ref[n], preferred_element_type=F32)
        zn = jax.nn.sigmoid(g_ref[...].astype(F32)) * zn
        z = zn if z is None else z + zn
    contrib = jnp.dot(z.astype(BF16), wo_ref[...], preferred_element_type=F32)

    @pl.when(j == 0)
    def _():
        o_ref[...] = contrib

    @pl.when(j > 0)
    def _():
        o_ref[...] += contrib

    @pl.when(j == nj - 1)
    def _():
        gate = _row_select(i, tpb, tm, gvc_ref[0, 0], gvb_ref[0, 0])
        o_ref[...] = _layer_norm(ALPHA * x_ref[...] + gate * o_ref[...], lng_ref[...], lnb_ref[...])


def _merge(ya, yb, yc, p, wb, wo, x, ada_l, ln_g, ln_b):
    tm, tj = TM_MERGE, TJ_MERGE
    tpb = T_TOK // tm
    nj = D_MODEL // tj
    ybs = pl.BlockSpec((tm, BRANCH_W), lambda i, j: (i, 0))
    gate = lambda n: pl.BlockSpec((tm, tj), lambda i, j: (i, (P_GATE + n * D_MODEL) // tj + j))
    row = pl.BlockSpec((tm, D_MODEL), lambda i, j: (i, 0))
    vec = pl.BlockSpec((1, D_MODEL), lambda i, j: (0, 0))
    return pl.pallas_call(
        functools.partial(_merge_kernel, tpb=tpb, tm=tm, nj=nj),
        grid=(ROWS // tm, nj),
        in_specs=[ybs, ybs, ybs, gate(0), gate(1), gate(2),
                  pl.BlockSpec((N_BRANCH, BRANCH_W, tj), lambda i, j: (0, 0, j)),
                  pl.BlockSpec((tj, D_MODEL), lambda i, j: (j, 0)),
                  row] + _vec_specs(2, tpb) + [vec, vec],
        out_specs=row,
        out_shape=jax.ShapeDtypeStruct((ROWS, D_MODEL), F32),
        compiler_params=_params(("arbitrary", "arbitrary")),
        name="merge",
    )(ya, yb, yc, p, p, p, wb, wo, x, ada_l, ada_l, ln_g.reshape(1, -1), ln_b.reshape(1, -1))


def _ffn_kernel(x_ref, scb_ref, scc_ref, shb_ref, shc_ref, gvb_ref, gvc_ref, wg_ref, wu_ref, wo_ref,
                lng_ref, lnb_ref, o_ref, u_ref, *, tpb, tm, nj):
    i = pl.program_id(0)
    j = pl.program_id(1)

    @pl.when(j == 0)
    def _():
        sc = _row_select(i, tpb, tm, scc_ref[0, 0], scb_ref[0, 0])
        sh = _row_select(i, tpb, tm, shc_ref[0, 0], shb_ref[0, 0])
        u_ref[...] = (x_ref[...] * sc + sh).astype(BF16)

    u = u_ref[...]
    gt = jnp.dot(u, wg_ref[...], preferred_element_type=F32)
    up = jnp.dot(u, wu_ref[...], preferred_element_type=F32)
    h = (gt * jax.nn.sigmoid(gt) * up).astype(BF16)
    contrib = jnp.dot(h, wo_ref[...], preferred_element_type=F32)

    @pl.when(j == 0)
    def _():
        o_ref[...] = contrib

    @pl.when(j > 0)
    def _():
        o_ref[...] += contrib

    @pl.when(j == nj - 1)
    def _():
        gate = _row_select(i, tpb, tm, gvc_ref[0, 0], gvb_ref[0, 0])
        o_ref[...] = _layer_norm(ALPHA * x_ref[...] + gate * o_ref[...], lng_ref[...], lnb_ref[...])


def _ffn(x, ada_l, w_in, w_out, ln_g, ln_b):
    tm, tj = TM_FFN, TJ_FFN
    tpb = T_TOK // tm
    nj = FFN_HIDDEN // tj
    row = pl.BlockSpec((tm, D_MODEL), lambda i, j: (i, 0))
    vec = pl.BlockSpec((1, D_MODEL), lambda i, j: (0, 0))
    return pl.pallas_call(
        functools.partial(_ffn_kernel, tpb=tpb, tm=tm, nj=nj),
        grid=(ROWS // tm, nj),
        in_specs=[row] + _vec_specs(4, tpb) + _vec_specs(3, tpb) + _vec_specs(5, tpb) + [
            pl.BlockSpec((D_MODEL, tj), lambda i, j: (0, j)),
            pl.BlockSpec((D_MODEL, tj), lambda i, j: (0, nj + j)),
            pl.BlockSpec((tj, D_MODEL), lambda i, j: (j, 0)),
            vec, vec],
        out_specs=row,
        out_shape=jax.ShapeDtypeStruct((ROWS, D_MODEL), F32),
        scratch_shapes=[pltpu.VMEM((tm, D_MODEL), BF16)],
        compiler_params=_params(("arbitrary", "arbitrary")),
        name="ffn",
    )(x, ada_l, ada_l, ada_l, ada_l, ada_l, ada_l, w_in, w_in, w_out, ln_g.reshape(1, -1), ln_b.reshape(1, -1))


def _rope_tables(both_halves):
    rot = SWA_HEAD_DIM
    half = rot // 2
    n_rows = SEQ // GRID_W
    row = jnp.repeat(jnp.arange(n_rows, dtype=F32), GRID_W)
    colp = jnp.tile(jnp.arange(GRID_W, dtype=F32), n_rows)
    inv = ROPE_BASE ** (-jnp.arange(0, half, 2, dtype=F32) / half)
    ang_r = row[:, None] * inv
    ang_c = colp[:, None] * inv
    cos = jnp.concatenate([jnp.cos(ang_r), jnp.cos(ang_r), jnp.cos(ang_c), jnp.cos(ang_c)], -1)
    sin = jnp.concatenate([-jnp.sin(ang_r), jnp.sin(ang_r), -jnp.sin(ang_c), jnp.sin(ang_c)], -1)
    if both_halves:
        cos = jnp.concatenate([cos, cos], -1)
        sin = jnp.concatenate([sin, sin], -1)
    else:
        cos = jnp.concatenate([cos, jnp.ones_like(cos)], -1)
        sin = jnp.concatenate([sin, jnp.zeros_like(sin)], -1)
    cos = jnp.concatenate([jnp.ones((CTX_LEN, LANES), F32), cos], 0)
    sin = jnp.concatenate([jnp.zeros((CTX_LEN, LANES), F32), sin], 0)
    return cos, sin


def _pad_mixer_weight(w_in_l):
    kr_end = P_BKR + MLA_ROPE
    pad = jnp.zeros((D_MODEL, P_CQ - kr_end), w_in_l.dtype)
    return jnp.concatenate([w_in_l[:, :kr_end], pad, w_in_l[:, kr_end:]], 1).astype(BF16)


def _mla_weights(w_q_up, w_kv_up):
    wq = w_q_up.reshape(MLA_Q_RANK, MLA_HEADS, MLA_NOPE + MLA_ROPE)
    wq = jnp.pad(wq, ((0, 0), (0, 0), (0, MLA_SLOT - MLA_NOPE - MLA_ROPE)))
    wkv = w_kv_up.reshape(MLA_KV_RANK, MLA_HEADS, MLA_NOPE + MLA_V)
    wk = wkv[:, :, :MLA_NOPE].reshape(MLA_KV_RANK, MLA_HEADS * MLA_NOPE)
    wv = wkv[:, :, MLA_NOPE:].reshape(MLA_KV_RANK, MLA_HEADS * MLA_V)
    return wq.reshape(MLA_Q_RANK, MLA_HEADS * MLA_SLOT).astype(BF16), wk.astype(BF16), wv.astype(BF16)


def kernel(x, c, ctx, c_ctx, w_ada, b_ada, w_in, conv_w, conv_b, lru_wr, lru_br, lru_wi, lru_bi, lru_lambda, mla_q_norm, mla_w_q_up, mla_kv_norm, mla_w_kv_up, swa_sinks, w_branch, w_out, ln1_g, ln1_b, w_ffn_in, w_ffn_out, ln2_g, ln2_b):
    cos_mla, sin_mla = _rope_tables(False)
    cos_swa, sin_swa = _rope_tables(True)
    cvec = jnp.concatenate([c, c_ctx[None], jnp.zeros((SUBLANES - BATCH - 1, D_MODEL), F32)], 0)
    ada = _ada(cvec, w_ada, b_ada).reshape(DEPTH, 6, SUBLANES, 1, D_MODEL)
    xs = jnp.concatenate([ctx, x], 1).reshape(ROWS, D_MODEL)
    for l in range(DEPTH):
        ada_l = ada[l]
        p = _mixin(xs, ada_l, _pad_mixer_weight(w_in[l]))
        ya = _lru(p, conv_w[l], conv_b[l], lru_wr[l].astype(BF16), lru_br[l], lru_wi[l].astype(BF16), lru_bi[l],
                  lru_lambda[l])
        wq_p, wk_p, wv_p = _mla_weights(mla_w_q_up[l], mla_w_kv_up[l])
        q, k, v = _mla_prep(p, mla_q_norm[l], wq_p, mla_kv_norm[l], wk_p, wv_p, cos_mla, sin_mla)
        yb = _mla_attn(q, k, v)
        yc = _swa(p, cos_swa, sin_swa, swa_sinks[l])
        x1 = _merge(ya, yb, yc, p, w_branch[l].astype(BF16), w_out[l].astype(BF16), xs, ada_l, ln1_g[l], ln1_b[l])
        xs = _ffn(x1, ada_l, w_ffn_in[l].astype(BF16), w_ffn_out[l].astype(BF16), ln2_g[l], ln2_b[l])
    return xs.reshape(BATCH, T_TOK, D_MODEL)[:, CTX_LEN:]
```

```python
import functools
import math

import jax
import jax.numpy as jnp
from jax import lax
from jax.experimental import pallas as pl
from jax.experimental.pallas import tpu as pltpu

F32 = jnp.float32
BF16 = jnp.bfloat16

D_MODEL = 2048
BATCH = 4
SEQ = 2048
DEPTH = 2
GRID_W = 64
CTX_LEN = 256
LRU_WIDTH = 1024
LRU_BLOCKS = 8
LRU_BLOCK_W = LRU_WIDTH // LRU_BLOCKS
LRU_C = 8.0
CONV_W = 4
MLA_HEADS = 8
MLA_Q_RANK = 512
MLA_KV_RANK = 256
MLA_NOPE = 128
MLA_ROPE = 64
MLA_V = 128
MLA_SCALE = (MLA_NOPE + MLA_ROPE) ** -0.5
SWA_HEADS = 16
SWA_KV_HEADS = 4
SWA_HEAD_DIM = 64
SWA_REP = SWA_HEADS // SWA_KV_HEADS
SWA_SCALE = SWA_HEAD_DIM ** -0.5
WINDOW = 128
N_BRANCH = 3
BRANCH_W = 1024
MIX_IN = 4416
FFN_HIDDEN = -(-8 * D_MODEL // (3 * 256)) * 256
ROPE_BASE = 10000.0
LN_EPS = 1e-5
RMS_EPS = 1e-6
ALPHA = (2 * DEPTH) ** 0.25

T_TOK = CTX_LEN + SEQ
ROWS = BATCH * T_TOK
LANES = 128
SUBLANES = 8
MLA_SLOT = 2 * LANES

P_AX, P_AG, P_BQ, P_BKV, P_BKR = 0, 1024, 2048, 2560, 2816
P_CQ, P_CK, P_CV, P_GATE = 3072, 4096, 4352, 4608
P_COLS = P_GATE + N_BRANCH * D_MODEL

VMEM_LIMIT = 56 * 1024 * 1024

TM_MIX = 1152
TN_MIX = 768
TM_MERGE = 576
TJ_MERGE = 512
TM_FFN = 768
TJ_FFN = 512
TM_PREP = 576
TQ_ATT = 256
TN_ADA = 2048
TM_MERGE_LATENT = 512
TM_FFN_LATENT = 1024
LOG2E = math.log2(math.e)


def _params(sem):
    return pltpu.CompilerParams(dimension_semantics=sem, vmem_limit_bytes=VMEM_LIMIT)


def _row_select(i, tiles_per_batch, tm, ctx_vec, batch_vec):
    rows = lax.broadcasted_iota(jnp.int32, (tm, 1), 0)
    n_ctx = jnp.where(i % tiles_per_batch == 0, CTX_LEN, 0)
    return jnp.where(rows < n_ctx, ctx_vec, batch_vec)


def _layer_norm(v, g, b):
    mu = jnp.mean(v, -1, keepdims=True)
    d = v - mu
    var = jnp.mean(d * d, -1, keepdims=True)
    return d * lax.rsqrt(var + LN_EPS) * g + b


def _rms_norm(v, g):
    return v * lax.rsqrt(jnp.mean(v * v, -1, keepdims=True) + RMS_EPS) * g


def _swap_pairs(x):
    lane = lax.broadcasted_iota(jnp.int32, (1, LANES), 1)
    first = (lane & 16) == 0
    return jnp.where(first, pltpu.roll(x, LANES - 16, 1), pltpu.roll(x, 16, 1))


def _rope(x, cos, sin):
    return x * cos + _swap_pairs(x) * sin


def _ada_kernel(c_ref, w_ref, b_ref, o_ref, *, chunks_per_vec):
    j = pl.program_id(1)
    c = c_ref[...]
    a = (c * jax.nn.sigmoid(c)).astype(BF16)
    acc = jnp.dot(a, w_ref[0].astype(BF16), preferred_element_type=F32)
    vec = j // chunks_per_vec
    one = jnp.where((vec == 1) | (vec == 4), 1.0, 0.0)
    o_ref[0, 0] = acc + b_ref[0] + one


def _ada(cvec, w_ada, b_ada):
    tn = TN_ADA
    cpv = D_MODEL // tn
    return pl.pallas_call(
        functools.partial(_ada_kernel, chunks_per_vec=cpv),
        grid=(DEPTH, 6 * cpv),
        in_specs=[
            pl.BlockSpec((SUBLANES, D_MODEL), lambda l, j: (0, 0)),
            pl.BlockSpec((1, D_MODEL, tn), lambda l, j: (l, 0, j)),
            pl.BlockSpec((1, 1, tn), lambda l, j: (l, 0, j)),
        ],
        out_specs=pl.BlockSpec((1, 1, SUBLANES, tn), lambda l, j: (l, j // cpv, 0, j % cpv)),
        out_shape=jax.ShapeDtypeStruct((DEPTH, 6, SUBLANES, D_MODEL), F32),
        compiler_params=_params(("arbitrary", "arbitrary")),
        name="ada",
    )(cvec, w_ada, b_ada.reshape(DEPTH, 1, 6 * D_MODEL))


def _vec_specs(l, vec, tiles_per_batch):
    return [
        pl.BlockSpec((None, 1, 1, 1, D_MODEL), lambda i, j: (l, vec, i // tiles_per_batch, 0, 0)),
        pl.BlockSpec((None, 1, 1, 1, D_MODEL), lambda i, j: (l, vec, BATCH, 0, 0)),
    ]


def _layer_vec(l):
    return pl.BlockSpec((None, 1, D_MODEL), lambda i, j: (l, 0, 0))


def _row_spec(tm, width, col_block, latent_only):
    if not latent_only:
        return pl.BlockSpec((tm, width), lambda i, j: (i, col_block(j)))
    tiles = SEQ // tm

    def index(i, j):
        row = (i // tiles) * T_TOK + CTX_LEN + (i % tiles) * tm
        return pl.multiple_of(row, CTX_LEN), pl.multiple_of(col_block(j) * width, LANES)

    return pl.BlockSpec((pl.Element(tm), pl.Element(width)), index)


def _mixin_kernel(x_ref, scb_ref, scc_ref, shb_ref, shc_ref, w_ref, o_ref, u_ref, *, tpb, tm):
    i = pl.program_id(0)

    @pl.when(pl.program_id(1) == 0)
    def _():
        sc = _row_select(i, tpb, tm, scc_ref[0, 0], scb_ref[0, 0])
        sh = _row_select(i, tpb, tm, shc_ref[0, 0], shb_ref[0, 0])
        u_ref[...] = (x_ref[...] * sc + sh).astype(BF16)

    o_ref[...] = jnp.dot(u_ref[...], w_ref[...], preferred_element_type=F32).astype(o_ref.dtype)


def _mixin(l, x, ada, w_p):
    tm, tn = TM_MIX, TN_MIX
    tpb = T_TOK // tm
    return pl.pallas_call(
        functools.partial(_mixin_kernel, tpb=tpb, tm=tm),
        grid=(ROWS // tm, P_COLS // tn),
        in_specs=[pl.BlockSpec((tm, D_MODEL), lambda i, j: (i, 0))]
        + _vec_specs(l, 1, tpb)
        + _vec_specs(l, 0, tpb)
        + [pl.BlockSpec((None, D_MODEL, tn), lambda i, j: (l, 0, j))],
        out_specs=pl.BlockSpec((tm, tn), lambda i, j: (i, j)),
        out_shape=jax.ShapeDtypeStruct((ROWS, P_COLS), BF16),
        scratch_shapes=[pltpu.VMEM((tm, D_MODEL), BF16)],
        compiler_params=_params(("arbitrary", "arbitrary")),
        name="mix_in",
    )(x, ada, ada, ada, ada, w_p)


def _lru_kernel(ax_ref, ag_ref, cw_ref, cb_ref, wr_ref, br_ref, wi_ref, bi_ref, lam_ref, y_ref,
                a_s, b_s, h_s):
    n_tok = T_TOK
    x = ax_ref[...].astype(F32)
    t = lax.broadcasted_iota(jnp.int32, (n_tok, 1), 0)
    is_ctx = t < CTX_LEN
    t_loc = jnp.where(is_ctx, t, t - CTX_LEN)
    seg_len = jnp.where(is_ctx, CTX_LEN, SEQ)
    xm1 = jnp.where(t_loc >= 1, pltpu.roll(x, 1, 0), 0.0)
    xm2 = jnp.where(t_loc >= 2, pltpu.roll(x, 2, 0), 0.0)
    xp1 = jnp.where(t_loc <= seg_len - 2, pltpu.roll(x, n_tok - 1, 0), 0.0)
    w = cw_ref[...]
    xc = xm2 * w[0:1] + xm1 * w[1:2] + x * w[2:3] + xp1 * w[3:4] + cb_ref[...]
    xb = xc.astype(BF16)
    sub = t & (SUBLANES - 1)

    for d in range(2):
        r = jax.nn.sigmoid(jnp.dot(xb, wr_ref[d, 0], preferred_element_type=F32) + br_ref[d])
        ig = jax.nn.sigmoid(jnp.dot(xb, wi_ref[d, 0], preferred_element_type=F32) + bi_ref[d])
        z = -lam_ref[d]
        softplus = jnp.maximum(z, 0.0) + jnp.log1p(jnp.exp(-jnp.abs(z)))
        log_a = (-LRU_C) * r * softplus
        a = jnp.exp(log_a)
        bb = jnp.sqrt(jnp.tanh(-log_a) * (a * a + 1.0)) * ig * xc
        for s in (1, 2, 4):
            if d == 0:
                m = sub >= s
                shift = s
            else:
                m = sub < SUBLANES - s
                shift = n_tok - s
            a_sh = pltpu.roll(a, shift, 0)
            b_sh = pltpu.roll(bb, shift, 0)
            bb = jnp.where(m, bb + a * b_sh, bb)
            a = jnp.where(m, a * a_sh, a)
        a_s[d] = a
        b_s[d] = bb

    zero = jnp.zeros((1, LANES), F32)

    def fwd(g, h):
        rows = pl.ds(pl.multiple_of(g * SUBLANES, SUBLANES), SUBLANES)
        hg = b_s[0, rows, :] + a_s[0, rows, :] * h
        h_s[rows, :] = hg
        return hg[SUBLANES - 1:SUBLANES]

    lax.fori_loop(0, n_tok // SUBLANES, fwd, zero, unroll=4)

    def bwd(k, h, *, last):
        g = last - k
        rows = pl.ds(pl.multiple_of(g * SUBLANES, SUBLANES), SUBLANES)
        hg = b_s[1, rows, :] + a_s[1, rows, :] * h
        h_s[rows, :] += hg
        return hg[0:1]

    g_ctx = CTX_LEN // SUBLANES
    h_c = lax.fori_loop(0, g_ctx, functools.partial(bwd, last=g_ctx - 1), zero, unroll=4)
    lax.fori_loop(0, SEQ // SUBLANES, functools.partial(bwd, last=n_tok // SUBLANES - 1), h_c, unroll=4)

    g = ag_ref[...].astype(F32)
    cdf = 0.5 * (1.0 + jnp.tanh(math.sqrt(2.0 / math.pi) * (g + 0.044715 * (g * g * g))))
    y_ref[...] = (h_s[...] * (g * cdf)).astype(y_ref.dtype)


def _lru(l, p, conv_w, conv_b, wr, br, wi, bi, lam):
    nb = LRU_BLOCKS
    col = lambda off: (lambda b, n: (b, off // LANES + n))
    vec2 = pl.BlockSpec((None, 2, 1, LANES), lambda b, n: (l, 0, 0, n))
    mat = pl.BlockSpec((None, 2, 1, LRU_BLOCK_W, LRU_BLOCK_W), lambda b, n: (l, 0, n, 0, 0))
    return pl.pallas_call(
        _lru_kernel,
        grid=(BATCH, nb),
        in_specs=[
            pl.BlockSpec((T_TOK, LANES), col(P_AX)),
            pl.BlockSpec((T_TOK, LANES), col(P_AG)),
            pl.BlockSpec((None, CONV_W, LANES), lambda b, n: (l, 0, n)),
            pl.BlockSpec((None, 1, LANES), lambda b, n: (l, 0, n)),
            mat, vec2, mat, vec2, vec2,
        ],
        out_specs=pl.BlockSpec((T_TOK, LANES), lambda b, n: (b, n)),
        out_shape=jax.ShapeDtypeStruct((ROWS, LRU_WIDTH), BF16),
        scratch_shapes=[
            pltpu.VMEM((2, T_TOK, LANES), F32),
            pltpu.VMEM((2, T_TOK, LANES), F32),
            pltpu.VMEM((T_TOK, LANES), F32),
        ],
        compiler_params=_params(("arbitrary", "arbitrary")),
        name="rg_lru",
    )(p, p, conv_w, conv_b.reshape(DEPTH, 1, LRU_WIDTH), wr, br.reshape(DEPTH, 2, 1, LRU_WIDTH), wi,
      bi.reshape(DEPTH, 2, 1, LRU_WIDTH), lam.reshape(DEPTH, 2, 1, LRU_WIDTH))


def _mla_prep_kernel(bq_ref, bkv_ref, bkr_ref, qn_ref, wq_ref, kvn_ref, wk_ref, wv_ref, cos_ref, sin_ref,
                     q_out, k_out, v_out):
    cos = cos_ref[...]
    sin = sin_ref[...]
    qn = _rms_norm(bq_ref[...].astype(F32), qn_ref[...])
    q = jnp.dot(qn.astype(BF16), wq_ref[...], preferred_element_type=F32) * (MLA_SCALE * LOG2E)
    kvn = _rms_norm(bkv_ref[...].astype(F32), kvn_ref[...]).astype(BF16)
    kn = jnp.dot(kvn, wk_ref[...], preferred_element_type=F32).astype(BF16)
    v_out[...] = jnp.dot(kvn, wv_ref[...], preferred_element_type=F32).astype(BF16)
    kr = _rope(bkr_ref[...].astype(F32), cos, sin).astype(BF16)
    for h in range(MLA_HEADS):
        lo = MLA_SLOT * h
        q_out[:, lo:lo + LANES] = q[:, lo:lo + LANES].astype(BF16)
        q_out[:, lo + LANES:lo + MLA_SLOT] = _rope(q[:, lo + LANES:lo + MLA_SLOT], cos, sin).astype(BF16)
        k_out[:, lo:lo + LANES] = kn[:, LANES * h:LANES * (h + 1)]
        k_out[:, lo + LANES:lo + MLA_SLOT] = kr


def _mla_prep(l, p, q_norm, wq_p, kv_norm, wk_p, wv_p, cos, sin):
    tm = TM_PREP
    tpb = T_TOK // tm
    full = lambda shape: pl.BlockSpec((None,) + shape, lambda i: (l,) + (0,) * len(shape))
    return pl.pallas_call(
        _mla_prep_kernel,
        grid=(ROWS // tm,),
        in_specs=[
            pl.BlockSpec((tm, MLA_Q_RANK), lambda i: (i, P_BQ // MLA_Q_RANK)),
            pl.BlockSpec((tm, MLA_KV_RANK), lambda i: (i, P_BKV // MLA_KV_RANK)),
            pl.BlockSpec((tm, LANES), lambda i: (i, P_BKR // LANES)),
            full((1, MLA_Q_RANK)),
            full((MLA_Q_RANK, MLA_HEADS * MLA_SLOT)),
            full((1, MLA_KV_RANK)),
            full((MLA_KV_RANK, MLA_HEADS * MLA_NOPE)),
            full((MLA_KV_RANK, MLA_HEADS * MLA_V)),
            pl.BlockSpec((tm, LANES), lambda i: (i % tpb, 0)),
            pl.BlockSpec((tm, LANES), lambda i: (i % tpb, 0)),
        ],
        out_specs=[
            pl.BlockSpec((tm, MLA_HEADS * MLA_SLOT), lambda i: (i, 0)),
            pl.BlockSpec((tm, MLA_HEADS * MLA_SLOT), lambda i: (i, 0)),
            pl.BlockSpec((tm, MLA_HEADS * MLA_V), lambda i: (i, 0)),
        ],
        out_shape=[
            jax.ShapeDtypeStruct((ROWS, MLA_HEADS * MLA_SLOT), BF16),
            jax.ShapeDtypeStruct((ROWS, MLA_HEADS * MLA_SLOT), BF16),
            jax.ShapeDtypeStruct((ROWS, MLA_HEADS * MLA_V), BF16),
        ],
        compiler_params=_params(("arbitrary",)),
        name="mla_prep",
    )(p, p, p, q_norm.reshape(DEPTH, 1, -1), wq_p, kv_norm.reshape(DEPTH, 1, -1), wk_p, wv_p, cos, sin)


def _mla_attn_kernel(q_ref, k_ref, v_ref, o_ref):
    def attend(n_keys):
        for h in range(MLA_HEADS):
            q = q_ref[:, MLA_SLOT * h:MLA_SLOT * (h + 1)]
            k = k_ref[0:n_keys, MLA_SLOT * h:MLA_SLOT * (h + 1)]
            s = lax.dot_general(q, k, (((1,), (1,)), ((), ())), preferred_element_type=F32)
            m = jnp.max(s, -1, keepdims=True)
            p = jnp.exp2(s - m)
            l = jnp.sum(p, -1, keepdims=True)
            o = jnp.dot(p.astype(BF16), v_ref[0:n_keys, MLA_V * h:MLA_V * (h + 1)], preferred_element_type=F32)
            o_ref[:, MLA_V * h:MLA_V * (h + 1)] = (o / l).astype(o_ref.dtype)

    @pl.when(pl.program_id(1) == 0)
    def _():
        attend(CTX_LEN)

    @pl.when(pl.program_id(1) > 0)
    def _():
        attend(T_TOK)


def _mla_attn(q, k, v):
    tq = TQ_ATT
    assert tq == CTX_LEN
    tpb = T_TOK // tq
    return pl.pallas_call(
        _mla_attn_kernel,
        grid=(BATCH, tpb),
        in_specs=[
            pl.BlockSpec((tq, MLA_HEADS * MLA_SLOT), lambda b, i: (b * tpb + i, 0)),
            pl.BlockSpec((T_TOK, MLA_HEADS * MLA_SLOT), lambda b, i: (b, 0)),
            pl.BlockSpec((T_TOK, MLA_HEADS * MLA_V), lambda b, i: (b, 0)),
        ],
        out_specs=pl.BlockSpec((tq, MLA_HEADS * MLA_V), lambda b, i: (b * tpb + i, 0)),
        out_shape=jax.ShapeDtypeStruct((ROWS, MLA_HEADS * MLA_V), BF16),
        compiler_params=_params(("arbitrary", "arbitrary")),
        name="mla_attn",
    )(q, k, v)


def _dup_half(x, upper):
    lane = lax.broadcasted_iota(jnp.int32, (1, LANES), 1)
    r = pltpu.roll(x, LANES // 2, 1)
    out = jnp.where(lane < LANES // 2, r, x) if upper else jnp.where(lane < LANES // 2, x, r)
    return out.astype(BF16)


def _swa_kernel(q_ref, k_ref, v_ref, cos_ref, sin_ref, sink_ref, o_ref, *, layer):
    tq = TQ_ATT
    band = tq + 2 * WINDOW
    qt = pl.program_id(1)
    q_row = pl.multiple_of(qt * tq, tq)
    lane = lax.broadcasted_iota(jnp.int32, (1, LANES), 1)
    low = lane < LANES // 2
    cos_q = cos_ref[pl.ds(q_row, tq), :]
    sin_q = sin_ref[pl.ds(q_row, tq), :]

    def q_block(c):
        x = q_ref[:, LANES * c:LANES * (c + 1)].astype(F32)
        return _rope(x, cos_q, sin_q) * (SWA_SCALE * LOG2E)

    def softmax_out(qm, sink, keys, vals, masks):
        ss = []
        for kk, mk in zip(keys, masks):
            s = lax.dot_general(qm, kk, (((1,), (1,)), ((), ())), preferred_element_type=F32)
            ss.append(s if mk is None else jnp.where(mk, s, -jnp.inf))
        m = jnp.maximum(functools.reduce(jnp.maximum, [jnp.max(s, -1, keepdims=True) for s in ss]), sink)
        ps = [jnp.exp2(s - m) for s in ss]
        l = functools.reduce(jnp.add, [jnp.sum(p, -1, keepdims=True) for p in ps]) + jnp.exp2(sink - m)
        o = functools.reduce(jnp.add, [jnp.dot(p.astype(BF16), vv, preferred_element_type=F32)
                                       for p, vv in zip(ps, vals)])
        return o / l

    def run(keys_of, vals_of, masks):
        for g in range(SWA_KV_HEADS):
            keys = keys_of(g)
            vals = vals_of(g)
            for c in (2 * g, 2 * g + 1):
                qc = q_block(c)
                o_lo = softmax_out(jnp.where(low, qc, 0.0).astype(BF16), sink_ref[layer, 2 * c] * LOG2E, keys, vals, masks)
                o_hi = softmax_out(jnp.where(low, 0.0, qc).astype(BF16), sink_ref[layer, 2 * c + 1] * LOG2E, keys, vals, masks)
                o_ref[:, LANES * c:LANES * (c + 1)] = jnp.where(low, o_lo, o_hi).astype(o_ref.dtype)

    def ctx_kv(ref, g):
        return _dup_half(ref[0:CTX_LEN, LANES * (g // 2):LANES * (g // 2 + 1)].astype(F32), g % 2 == 1)

    @pl.when(qt == 0)
    def _():
        run(lambda g: [ctx_kv(k_ref, g)], lambda g: [ctx_kv(v_ref, g)], [None])

    @pl.when(qt > 0)
    def _():
        start = pl.multiple_of(jnp.minimum(q_row - WINDOW, T_TOK - band), LANES)
        qpos = q_row - CTX_LEN + lax.broadcasted_iota(jnp.int32, (tq, 1), 0)
        kpos = start - CTX_LEN + lax.broadcasted_iota(jnp.int32, (1, band), 1)
        valid = (jnp.abs(kpos - qpos) <= WINDOW) & (kpos >= 0)
        cos_k = cos_ref[pl.ds(start, band), :]
        sin_k = sin_ref[pl.ds(start, band), :]

        def band_k(g):
            kb = k_ref[pl.ds(start, band), LANES * (g // 2):LANES * (g // 2 + 1)].astype(F32)
            return _dup_half(_rope(kb, cos_k, sin_k), g % 2 == 1)

        def band_v(g):
            return _dup_half(v_ref[pl.ds(start, band), LANES * (g // 2):LANES * (g // 2 + 1)].astype(F32), g % 2 == 1)

        run(lambda g: [ctx_kv(k_ref, g), band_k(g)], lambda g: [ctx_kv(v_ref, g), band_v(g)], [None, valid])


def _swa(l, p, cos, sin, sinks):
    tq = TQ_ATT
    tpb = T_TOK // tq
    kvw = SWA_KV_HEADS * SWA_HEAD_DIM
    return pl.pallas_call(
        functools.partial(_swa_kernel, layer=l),
        grid=(BATCH, tpb),
        in_specs=[
            pl.BlockSpec((tq, SWA_HEADS * SWA_HEAD_DIM), lambda b, i: (b * tpb + i, P_CQ // (SWA_HEADS * SWA_HEAD_DIM))),
            pl.BlockSpec((T_TOK, kvw), lambda b, i: (b, P_CK // kvw)),
            pl.BlockSpec((T_TOK, kvw), lambda b, i: (b, P_CV // kvw)),
            pl.BlockSpec((T_TOK, LANES), lambda b, i: (0, 0)),
            pl.BlockSpec((T_TOK, LANES), lambda b, i: (0, 0)),
            pl.BlockSpec(memory_space=pltpu.SMEM),
        ],
        out_specs=pl.BlockSpec((tq, SWA_HEADS * SWA_HEAD_DIM), lambda b, i: (b * tpb + i, 0)),
        out_shape=jax.ShapeDtypeStruct((ROWS, SWA_HEADS * SWA_HEAD_DIM), BF16),
        compiler_params=_params(("arbitrary", "arbitrary")),
        name="swa",
    )(p, p, p, cos, sin, sinks)


def _gate_rows(i, tpb, tm, gvc_ref, gvb_ref, has_ctx):
    return _row_select(i, tpb, tm, gvc_ref[0, 0], gvb_ref[0, 0]) if has_ctx else gvb_ref[0, 0]


def _merge_kernel(ya_ref, yb_ref, yc_ref, ga_ref, gb_ref, gc_ref, wb_ref, wo_ref, x_ref, gvb_ref, gvc_ref,
                  lng_ref, lnb_ref, o_ref, *, tpb, tm, nj, has_ctx):
    i = pl.program_id(0)
    j = pl.program_id(1)

    @pl.when(j == 0)
    def _():
        o_ref[...] = jnp.zeros_like(o_ref)

    z = None
    for n, (y_ref, g_ref) in enumerate(((ya_ref, ga_ref), (yb_ref, gb_ref), (yc_ref, gc_ref))):
        zn = jnp.dot(y_ref[...], wb_ref[n], preferred_element_type=F32)
        zn = jax.nn.sigmoid(g_ref[...].astype(F32)) * zn
        z = zn if z is None else z + zn
    o_ref[...] += jnp.dot(z.astype(BF16), wo_ref[...], preferred_element_type=F32)

    @pl.when(j == nj - 1)
    def _():
        gate = _gate_rows(i, tpb, tm, gvc_ref, gvb_ref, has_ctx)
        o_ref[...] = _layer_norm(ALPHA * x_ref[...] + gate * o_ref[...], lng_ref[...], lnb_ref[...])


def _merge(l, ya, yb, yc, p, wb, wo, x, ada, ln_g, ln_b, latent_only):
    tm = TM_MERGE_LATENT if latent_only else TM_MERGE
    tj = TJ_MERGE
    tpb = (SEQ if latent_only else T_TOK) // tm
    n_rows = BATCH * SEQ if latent_only else ROWS
    nj = D_MODEL // tj
    ybs = _row_spec(tm, BRANCH_W, lambda j: 0, latent_only)
    gate = lambda n: _row_spec(tm, tj, lambda j: (P_GATE + n * D_MODEL) // tj + j, latent_only)
    return pl.pallas_call(
        functools.partial(_merge_kernel, tpb=tpb, tm=tm, nj=nj, has_ctx=not latent_only),
        grid=(n_rows // tm, nj),
        in_specs=[ybs, ybs, ybs, gate(0), gate(1), gate(2),
                  pl.BlockSpec((None, N_BRANCH, BRANCH_W, tj), lambda i, j: (l, 0, 0, j)),
                  pl.BlockSpec((None, tj, D_MODEL), lambda i, j: (l, j, 0)),
                  _row_spec(tm, D_MODEL, lambda j: 0, latent_only)]
        + _vec_specs(l, 2, tpb) + [_layer_vec(l), _layer_vec(l)],
        out_specs=pl.BlockSpec((tm, D_MODEL), lambda i, j: (i, 0)),
        out_shape=jax.ShapeDtypeStruct((n_rows, D_MODEL), F32),
        compiler_params=_params(("arbitrary", "arbitrary")),
        name="merge",
    )(ya, yb, yc, p, p, p, wb, wo, x, ada, ada, ln_g.reshape(DEPTH, 1, -1), ln_b.reshape(DEPTH, 1, -1))


def _ffn_kernel(x_ref, scb_ref, scc_ref, shb_ref, shc_ref, gvb_ref, gvc_ref, wg_ref, wu_ref, wo_ref,
                lng_ref, lnb_ref, o_ref, u_ref, *, tpb, tm, nj, has_ctx):
    i = pl.program_id(0)
    j = pl.program_id(1)

    @pl.when(j == 0)
    def _():
        sc = _gate_rows(i, tpb, tm, scc_ref, scb_ref, has_ctx)
        sh = _gate_rows(i, tpb, tm, shc_ref, shb_ref, has_ctx)
        u_ref[...] = (x_ref[...] * sc + sh).astype(BF16)
        o_ref[...] = jnp.zeros_like(o_ref)

    u = u_ref[...]
    gt = jnp.dot(u, wg_ref[...], preferred_element_type=F32)
    up = jnp.dot(u, wu_ref[...], preferred_element_type=F32)
    h = (gt * jax.nn.sigmoid(gt) * up).astype(BF16)
    o_ref[...] += jnp.dot(h, wo_ref[...], preferred_element_type=F32)

    @pl.when(j == nj - 1)
    def _():
        gate = _gate_rows(i, tpb, tm, gvc_ref, gvb_ref, has_ctx)
        o_ref[...] = _layer_norm(ALPHA * x_ref[...] + gate * o_ref[...], lng_ref[...], lnb_ref[...])


def _ffn(l, x, ada, w_in, w_out, ln_g, ln_b, latent_only):
    tm = TM_FFN_LATENT if latent_only else TM_FFN
    tj = TJ_FFN
    tpb = (SEQ if latent_only else T_TOK) // tm
    n_rows = x.shape[0]
    nj = FFN_HIDDEN // tj
    row = pl.BlockSpec((tm, D_MODEL), lambda i, j: (i, 0))
    row_in = pl.BlockSpec((tm, D_MODEL), lambda i, j: (i, 0), pipeline_mode=pl.Buffered(1))
    return pl.pallas_call(
        functools.partial(_ffn_kernel, tpb=tpb, tm=tm, nj=nj, has_ctx=not latent_only),
        grid=(n_rows // tm, nj),
        in_specs=[row_in] + _vec_specs(l, 4, tpb) + _vec_specs(l, 3, tpb) + _vec_specs(l, 5, tpb) + [
            pl.BlockSpec((None, D_MODEL, tj), lambda i, j: (l, 0, j)),
            pl.BlockSpec((None, D_MODEL, tj), lambda i, j: (l, 0, nj + j)),
            pl.BlockSpec((None, tj, D_MODEL), lambda i, j: (l, j, 0)),
            _layer_vec(l), _layer_vec(l)],
        out_specs=row,
        out_shape=jax.ShapeDtypeStruct((n_rows, D_MODEL), F32),
        scratch_shapes=[pltpu.VMEM((tm, D_MODEL), BF16)],
        compiler_params=_params(("arbitrary", "arbitrary")),
        name="ffn",
    )(x, ada, ada, ada, ada, ada, ada, w_in, w_in, w_out, ln_g.reshape(DEPTH, 1, -1), ln_b.reshape(DEPTH, 1, -1))


def _rope_tables(both_halves):
    rot = SWA_HEAD_DIM
    half = rot // 2
    n_rows = SEQ // GRID_W
    row = jnp.repeat(jnp.arange(n_rows, dtype=F32), GRID_W)
    colp = jnp.tile(jnp.arange(GRID_W, dtype=F32), n_rows)
    inv = ROPE_BASE ** (-jnp.arange(0, half, 2, dtype=F32) / half)
    ang_r = row[:, None] * inv
    ang_c = colp[:, None] * inv
    cos = jnp.concatenate([jnp.cos(ang_r), jnp.cos(ang_r), jnp.cos(ang_c), jnp.cos(ang_c)], -1)
    sin = jnp.concatenate([-jnp.sin(ang_r), jnp.sin(ang_r), -jnp.sin(ang_c), jnp.sin(ang_c)], -1)
    if both_halves:
        cos = jnp.concatenate([cos, cos], -1)
        sin = jnp.concatenate([sin, sin], -1)
    else:
        cos = jnp.concatenate([cos, jnp.ones_like(cos)], -1)
        sin = jnp.concatenate([sin, jnp.zeros_like(sin)], -1)
    cos = jnp.concatenate([jnp.ones((CTX_LEN, LANES), F32), cos], 0)
    sin = jnp.concatenate([jnp.zeros((CTX_LEN, LANES), F32), sin], 0)
    return cos, sin


def _pad_mixer_weight_kernel(w_ref, o_ref):
    kr_end = P_BKR + MLA_ROPE
    o_ref[:, :kr_end] = w_ref[:, :kr_end].astype(BF16)
    o_ref[:, kr_end:P_CQ] = jnp.zeros((o_ref.shape[0], P_CQ - kr_end), BF16)
    o_ref[:, P_CQ:] = w_ref[:, kr_end:].astype(BF16)


def _pad_mixer_weight(w_in):
    rows = 256
    in_cols = w_in.shape[-1]
    return pl.pallas_call(
        _pad_mixer_weight_kernel,
        grid=(DEPTH, D_MODEL // rows),
        in_specs=[pl.BlockSpec((None, rows, in_cols), lambda l, i: (l, i, 0))],
        out_specs=pl.BlockSpec((None, rows, P_COLS), lambda l, i: (l, i, 0)),
        out_shape=jax.ShapeDtypeStruct((DEPTH, D_MODEL, P_COLS), BF16),
        compiler_params=_params(("arbitrary", "arbitrary")),
        name="pad_w_in",
    )(w_in)


def _mla_weights(w_q_up, w_kv_up):
    wq = w_q_up.reshape(DEPTH, MLA_Q_RANK, MLA_HEADS, MLA_NOPE + MLA_ROPE)
    wq = jnp.pad(wq, ((0, 0), (0, 0), (0, 0), (0, MLA_SLOT - MLA_NOPE - MLA_ROPE)))
    wkv = w_kv_up.reshape(DEPTH, MLA_KV_RANK, MLA_HEADS, MLA_NOPE + MLA_V)
    wk = wkv[..., :MLA_NOPE].reshape(DEPTH, MLA_KV_RANK, MLA_HEADS * MLA_NOPE)
    wv = wkv[..., MLA_NOPE:].reshape(DEPTH, MLA_KV_RANK, MLA_HEADS * MLA_V)
    return wq.reshape(DEPTH, MLA_Q_RANK, MLA_HEADS * MLA_SLOT).astype(BF16), wk.astype(BF16), wv.astype(BF16)


def kernel(x, c, ctx, c_ctx, w_ada, b_ada, w_in, conv_w, conv_b, lru_wr, lru_br, lru_wi, lru_bi, lru_lambda, mla_q_norm, mla_w_q_up, mla_kv_norm, mla_w_kv_up, swa_sinks, w_branch, w_out, ln1_g, ln1_b, w_ffn_in, w_ffn_out, ln2_g, ln2_b):
    cos_mla, sin_mla = _rope_tables(False)
    cos_swa, sin_swa = _rope_tables(True)
    cvec = jnp.concatenate([c, c_ctx[None], jnp.zeros((SUBLANES - BATCH - 1, D_MODEL), F32)], 0)
    ada = _ada(cvec, w_ada, b_ada).reshape(DEPTH, 6, SUBLANES, 1, D_MODEL)
    w_p = _pad_mixer_weight(w_in)
    wr, wi = lru_wr.astype(BF16), lru_wi.astype(BF16)
    wq_p, wk_p, wv_p = _mla_weights(mla_w_q_up, mla_w_kv_up)
    wb, wo = w_branch.astype(BF16), w_out.astype(BF16)
    wf_in, wf_out = w_ffn_in.astype(BF16), w_ffn_out.astype(BF16)
    xs = jnp.concatenate([ctx, x], 1).reshape(ROWS, D_MODEL)
    for l in range(DEPTH):
        last = l == DEPTH - 1
        p = _mixin(l, xs, ada, w_p)
        ya = _lru(l, p, conv_w, conv_b, wr, lru_br, wi, lru_bi, lru_lambda)
        q, k, v = _mla_prep(l, p, mla_q_norm, wq_p, mla_kv_norm, wk_p, wv_p, cos_mla, sin_mla)
        yb = _mla_attn(q, k, v)
        yc = _swa(l, p, cos_swa, sin_swa, swa_sinks)
        x1 = _merge(l, ya, yb, yc, p, wb, wo, xs, ada, ln1_g, ln1_b, last)
        xs = _ffn(l, x1, ada, wf_in, wf_out, ln2_g, ln2_b, last)
    return xs.reshape(BATCH, SEQ, D_MODEL)
```

```python
import functools
import math

import jax
import jax.numpy as jnp
from jax import lax
from jax.experimental import pallas as pl
from jax.experimental.pallas import tpu as pltpu

F32 = jnp.float32
BF16 = jnp.bfloat16

D_MODEL = 2048
BATCH = 4
SEQ = 2048
DEPTH = 2
GRID_W = 64
CTX_LEN = 256
LRU_WIDTH = 1024
LRU_BLOCKS = 8
LRU_BLOCK_W = LRU_WIDTH // LRU_BLOCKS
LRU_C = 8.0
CONV_W = 4
MLA_HEADS = 8
MLA_Q_RANK = 512
MLA_KV_RANK = 256
MLA_NOPE = 128
MLA_ROPE = 64
MLA_V = 128
MLA_SCALE = (MLA_NOPE + MLA_ROPE) ** -0.5
SWA_HEADS = 16
SWA_KV_HEADS = 4
SWA_HEAD_DIM = 64
SWA_REP = SWA_HEADS // SWA_KV_HEADS
SWA_SCALE = SWA_HEAD_DIM ** -0.5
WINDOW = 128
N_BRANCH = 3
BRANCH_W = 1024
MIX_IN = 4416
FFN_HIDDEN = -(-8 * D_MODEL // (3 * 256)) * 256
ROPE_BASE = 10000.0
LN_EPS = 1e-5
RMS_EPS = 1e-6
ALPHA = (2 * DEPTH) ** 0.25

T_TOK = CTX_LEN + SEQ
ROWS = BATCH * T_TOK
LANES = 128
SUBLANES = 8
MLA_SLOT = 2 * LANES

P_AX, P_AG, P_BQ, P_BKV, P_BKR = 0, 1024, 2048, 2560, 2816
P_CQ, P_CK, P_CV, P_GATE = 3072, 4096, 4352, 4608
P_COLS = P_GATE + N_BRANCH * D_MODEL

VMEM_LIMIT = 56 * 1024 * 1024

TM_MIX = 1152
TN_MIX = 768
TM_MERGE = 576
TJ_MERGE = 512
TM_FFN = 768
TJ_FFN = 512
TM_PREP = 576
TQ_ATT = 256
TN_ADA = 2048
TM_MERGE_LATENT = 512
TM_FFN_LATENT = 1024
LOG2E = math.log2(math.e)


def _params(sem):
    return pltpu.CompilerParams(dimension_semantics=sem, vmem_limit_bytes=VMEM_LIMIT)


def _row_select(i, tiles_per_batch, tm, ctx_vec, batch_vec):
    rows = lax.broadcasted_iota(jnp.int32, (tm, 1), 0)
    n_ctx = jnp.where(i % tiles_per_batch == 0, CTX_LEN, 0)
    return jnp.where(rows < n_ctx, ctx_vec, batch_vec)


def _layer_norm(v, g, b):
    mu = jnp.mean(v, -1, keepdims=True)
    d = v - mu
    var = jnp.mean(d * d, -1, keepdims=True)
    return d * lax.rsqrt(var + LN_EPS) * g + b


def _rms_norm(v, g):
    return v * lax.rsqrt(jnp.mean(v * v, -1, keepdims=True) + RMS_EPS) * g


def _swap_pairs(x):
    lane = lax.broadcasted_iota(jnp.int32, (1, LANES), 1)
    first = (lane & 16) == 0
    return jnp.where(first, pltpu.roll(x, LANES - 16, 1), pltpu.roll(x, 16, 1))


def _rope(x, cos, sin):
    return x * cos + _swap_pairs(x) * sin


def _ada_kernel(c_ref, w_ref, b_ref, o_ref, *, chunks_per_vec):
    j = pl.program_id(1)
    c = c_ref[...]
    a = (c * jax.nn.sigmoid(c)).astype(BF16)
    acc = jnp.dot(a, w_ref[0].astype(BF16), preferred_element_type=F32)
    vec = j // chunks_per_vec
    one = jnp.where((vec == 1) | (vec == 4), 1.0, 0.0)
    o_ref[0, 0] = acc + b_ref[0] + one


def _ada(cvec, w_ada, b_ada):
    tn = TN_ADA
    cpv = D_MODEL // tn
    return pl.pallas_call(
        functools.partial(_ada_kernel, chunks_per_vec=cpv),
        grid=(DEPTH, 6 * cpv),
        in_specs=[
            pl.BlockSpec((SUBLANES, D_MODEL), lambda l, j: (0, 0)),
            pl.BlockSpec((1, D_MODEL, tn), lambda l, j: (l, 0, j)),
            pl.BlockSpec((1, 1, tn), lambda l, j: (l, 0, j)),
        ],
        out_specs=pl.BlockSpec((1, 1, SUBLANES, tn), lambda l, j: (l, j // cpv, 0, j % cpv)),
        out_shape=jax.ShapeDtypeStruct((DEPTH, 6, SUBLANES, D_MODEL), F32),
        compiler_params=_params(("arbitrary", "arbitrary")),
        name="ada",
    )(cvec, w_ada, b_ada.reshape(DEPTH, 1, 6 * D_MODEL))


def _vec_specs(l, vec, tiles_per_batch):
    return [
        pl.BlockSpec((None, 1, 1, 1, D_MODEL), lambda i, j: (l, vec, i // tiles_per_batch, 0, 0)),
        pl.BlockSpec((None, 1, 1, 1, D_MODEL), lambda i, j: (l, vec, BATCH, 0, 0)),
    ]


def _layer_vec(l):
    return pl.BlockSpec((None, 1, D_MODEL), lambda i, j: (l, 0, 0))


def _row_spec(tm, width, col_block, latent_only):
    if not latent_only:
        return pl.BlockSpec((tm, width), lambda i, j: (i, col_block(j)))
    tiles = SEQ // tm

    def index(i, j):
        row = (i // tiles) * T_TOK + CTX_LEN + (i % tiles) * tm
        return pl.multiple_of(row, CTX_LEN), pl.multiple_of(col_block(j) * width, LANES)

    return pl.BlockSpec((pl.Element(tm), pl.Element(width)), index)


def _mixin_kernel(x_ref, scb_ref, scc_ref, shb_ref, shc_ref, w_ref, o_ref, u_ref, *, tpb, tm):
    i = pl.program_id(0)

    @pl.when(pl.program_id(1) == 0)
    def _():
        sc = _row_select(i, tpb, tm, scc_ref[0, 0], scb_ref[0, 0])
        sh = _row_select(i, tpb, tm, shc_ref[0, 0], shb_ref[0, 0])
        u_ref[...] = (x_ref[...] * sc + sh).astype(BF16)

    o_ref[...] = lax.dot_general(u_ref[...], w_ref[...], (((1,), (1,)), ((), ())),
                                 preferred_element_type=F32).astype(o_ref.dtype)


def _mixin(l, x, ada, w_t):
    tm, tn = TM_MIX, TN_MIX
    tpb = T_TOK // tm
    pad = P_CQ - (P_BKR + MLA_ROPE)
    assert P_CQ % tn == 0 and w_t.shape[1] + pad == P_COLS

    def w_index(i, j):
        row = j * tn - jnp.where(j >= P_CQ // tn, pad, 0)
        return l, pl.multiple_of(row, MLA_ROPE), 0

    return pl.pallas_call(
        functools.partial(_mixin_kernel, tpb=tpb, tm=tm),
        grid=(ROWS // tm, P_COLS // tn),
        in_specs=[pl.BlockSpec((tm, D_MODEL), lambda i, j: (i, 0))]
        + _vec_specs(l, 1, tpb)
        + _vec_specs(l, 0, tpb)
        + [pl.BlockSpec((None, pl.Element(tn), pl.Element(D_MODEL)), w_index)],
        out_specs=pl.BlockSpec((tm, tn), lambda i, j: (i, j)),
        out_shape=jax.ShapeDtypeStruct((ROWS, P_COLS), BF16),
        scratch_shapes=[pltpu.VMEM((tm, D_MODEL), BF16)],
        compiler_params=_params(("arbitrary", "arbitrary")),
        name="mix_in",
    )(x, ada, ada, ada, ada, w_t)


LRU_SEG_FWD = 292
LRU_SEG_BWD = 260
LRU_PAD_TOK = SUBLANES * LRU_SEG_FWD
assert LRU_PAD_TOK >= T_TOK and CTX_LEN + SUBLANES * LRU_SEG_BWD == LRU_PAD_TOK
assert LRU_SEG_FWD % 8 == 4 and LRU_SEG_BWD % 8 == 4


def _lru_kernel(ax_ref, ag_ref, cw_ref, cb_ref, wr_ref, br_ref, wi_ref, bi_ref, lam_ref, y_ref,
                a_s, b_s, h_s, p_s):
    n_tok = T_TOK
    x = ax_ref[...].astype(F32)
    t = lax.broadcasted_iota(jnp.int32, (n_tok, 1), 0)
    is_ctx = t < CTX_LEN
    t_loc = jnp.where(is_ctx, t, t - CTX_LEN)
    seg_len = jnp.where(is_ctx, CTX_LEN, SEQ)
    xm1 = jnp.where(t_loc >= 1, pltpu.roll(x, 1, 0), 0.0)
    xm2 = jnp.where(t_loc >= 2, pltpu.roll(x, 2, 0), 0.0)
    xp1 = jnp.where(t_loc <= seg_len - 2, pltpu.roll(x, n_tok - 1, 0), 0.0)
    w = cw_ref[...]
    xc = xm2 * w[0:1] + xm1 * w[1:2] + x * w[2:3] + xp1 * w[3:4] + cb_ref[...]
    xb = xc.astype(BF16)

    for d in range(2):
        r = jax.nn.sigmoid(jnp.dot(xb, wr_ref[d, 0], preferred_element_type=F32) + br_ref[d])
        ig = jax.nn.sigmoid(jnp.dot(xb, wi_ref[d, 0], preferred_element_type=F32) + bi_ref[d])
        z = -lam_ref[d]
        softplus = jnp.maximum(z, 0.0) + jnp.log1p(jnp.exp(-jnp.abs(z)))
        log_a = (-LRU_C) * r * softplus
        a = jnp.exp(log_a)
        v = jnp.tanh(-log_a) * (a * a + 1.0)
        mult = jnp.where(v > 0.0, v * lax.rsqrt(v), 0.0)
        a_s[d, 0:n_tok, :] = a
        b_s[d, 0:n_tok, :] = mult * ig * xc
        a_s[d, n_tok:LRU_PAD_TOK, :] = jnp.ones((LRU_PAD_TOK - n_tok, LANES), F32)
        b_s[d, n_tok:LRU_PAD_TOK, :] = jnp.zeros((LRU_PAD_TOK - n_tok, LANES), F32)

    a = a_s[1, 0:CTX_LEN, :]
    bb = b_s[1, 0:CTX_LEN, :]
    sub = lax.broadcasted_iota(jnp.int32, (CTX_LEN, 1), 0) & (SUBLANES - 1)
    for s in (1, 2, 4):
        m = sub < SUBLANES - s
        a_sh = pltpu.roll(a, CTX_LEN - s, 0)
        b_sh = pltpu.roll(bb, CTX_LEN - s, 0)
        bb = jnp.where(m, bb + a * b_sh, bb)
        a = jnp.where(m, a * a_sh, a)
    a_s[1, 0:CTX_LEN, :] = a
    b_s[1, 0:CTX_LEN, :] = bb

    def ctx_bwd(k, h):
        rows = pl.ds(pl.multiple_of((CTX_LEN // SUBLANES - 1 - k) * SUBLANES, SUBLANES), SUBLANES)
        hg = b_s[1, rows, :] + a_s[1, rows, :] * h
        b_s[1, rows, :] = hg
        return hg[0:1]

    h_ctx0 = lax.fori_loop(0, CTX_LEN // SUBLANES, ctx_bwd, jnp.zeros((1, LANES), F32), unroll=4)

    def seg_rows(base, s, seg):
        return pl.ds(base + s, SUBLANES, stride=seg)

    def fwd_step(s, h, p):
        rows = seg_rows(0, s, LRU_SEG_FWD)
        av = a_s[0, rows, :]
        h = av * h + b_s[0, rows, :]
        p = av * p
        h_s[0, rows, :] = h
        p_s[0, rows, :] = p
        return h, p

    def bwd_step(s, h, p):
        rows = seg_rows(CTX_LEN, LRU_SEG_BWD - 1 - s, LRU_SEG_BWD)
        av = a_s[1, rows, :]
        h = av * h + b_s[1, rows, :]
        p = av * p
        h_s[1, rows, :] = h
        p_s[1, rows, :] = p
        return h, p

    zeros8 = jnp.zeros((SUBLANES, LANES), F32)
    ones8 = jnp.ones((SUBLANES, LANES), F32)

    def both(s, c):
        return fwd_step(s, c[0], c[1]) + bwd_step(s, c[2], c[3])

    hf, pf, hb, pb = lax.fori_loop(0, LRU_SEG_BWD, both, (zeros8, ones8, zeros8, ones8), unroll=4)
    hf, pf = lax.fori_loop(LRU_SEG_BWD, LRU_SEG_FWD, lambda s, c: fwd_step(s, c[0], c[1]), (hf, pf), unroll=4)

    seg = lax.broadcasted_iota(jnp.int32, (SUBLANES, 1), 0)
    cf = zeros8
    cb = jnp.where(seg == SUBLANES - 1, h_ctx0, 0.0)
    for _ in range(SUBLANES - 1):
        cf = jnp.where(seg >= 1, pltpu.roll(hf + pf * cf, 1, 0), 0.0)
        cb = jnp.where(seg <= SUBLANES - 2, pltpu.roll(hb + pb * cb, SUBLANES - 1, 0), h_ctx0)

    def fix_fwd(s):
        rows = seg_rows(0, s, LRU_SEG_FWD)
        b_s[0, rows, :] = h_s[0, rows, :] + p_s[0, rows, :] * cf

    def fix_bwd(s):
        rows = seg_rows(CTX_LEN, s, LRU_SEG_BWD)
        b_s[1, rows, :] = h_s[1, rows, :] + p_s[1, rows, :] * cb

    def fix_both(s, c):
        fix_fwd(s)
        fix_bwd(s)
        return c

    def fix_tail(s, c):
        fix_fwd(s)
        return c

    lax.fori_loop(0, LRU_SEG_BWD, fix_both, 0, unroll=4)
    lax.fori_loop(LRU_SEG_BWD, LRU_SEG_FWD, fix_tail, 0, unroll=4)

    g = ag_ref[...].astype(F32)
    cdf = 0.5 * (1.0 + jnp.tanh(math.sqrt(2.0 / math.pi) * (g + 0.044715 * (g * g * g))))
    h_sum = b_s[0, 0:n_tok, :] + b_s[1, 0:n_tok, :]
    y_ref[...] = (h_sum * (g * cdf)).astype(y_ref.dtype)


def _lru(l, p, conv_w, conv_b, wr, br, wi, bi, lam):
    nb = LRU_BLOCKS
    col = lambda off: (lambda b, n: (b, off // LANES + n))
    vec2 = pl.BlockSpec((None, 2, 1, LANES), lambda b, n: (l, 0, 0, n))
    mat = pl.BlockSpec((None, 2, 1, LRU_BLOCK_W, LRU_BLOCK_W), lambda b, n: (l, 0, n, 0, 0))
    return pl.pallas_call(
        _lru_kernel,
        grid=(BATCH, nb),
        in_specs=[
            pl.BlockSpec((T_TOK, LANES), col(P_AX)),
            pl.BlockSpec((T_TOK, LANES), col(P_AG)),
            pl.BlockSpec((None, CONV_W, LANES), lambda b, n: (l, 0, n)),
            pl.BlockSpec((None, 1, LANES), lambda b, n: (l, 0, n)),
            mat, vec2, mat, vec2, vec2,
        ],
        out_specs=pl.BlockSpec((T_TOK, LANES), lambda b, n: (b, n)),
        out_shape=jax.ShapeDtypeStruct((ROWS, LRU_WIDTH), BF16),
        scratch_shapes=[
            pltpu.VMEM((2, LRU_PAD_TOK, LANES), F32),
            pltpu.VMEM((2, LRU_PAD_TOK, LANES), F32),
            pltpu.VMEM((2, LRU_PAD_TOK, LANES), F32),
            pltpu.VMEM((2, LRU_PAD_TOK, LANES), F32),
        ],
        compiler_params=_params(("arbitrary", "arbitrary")),
        name="rg_lru",
    )(p, p, conv_w, conv_b.reshape(DEPTH, 1, LRU_WIDTH), wr, br.reshape(DEPTH, 2, 1, LRU_WIDTH), wi,
      bi.reshape(DEPTH, 2, 1, LRU_WIDTH), lam.reshape(DEPTH, 2, 1, LRU_WIDTH))


def _mla_prep_kernel(bq_ref, bkv_ref, bkr_ref, qn_ref, wq_ref, kvn_ref, wk_ref, wv_ref, cos_ref, sin_ref,
                     q_out, k_out, v_out):
    cos = cos_ref[...]
    sin = sin_ref[...]
    qn = _rms_norm(bq_ref[...].astype(F32), qn_ref[...])
    q = jnp.dot(qn.astype(BF16), wq_ref[...], preferred_element_type=F32) * (MLA_SCALE * LOG2E)
    kvn = _rms_norm(bkv_ref[...].astype(F32), kvn_ref[...]).astype(BF16)
    kn = jnp.dot(kvn, wk_ref[...], preferred_element_type=F32).astype(BF16)
    v_out[...] = jnp.dot(kvn, wv_ref[...], preferred_element_type=F32).astype(BF16)
    lane = lax.broadcasted_iota(jnp.int32, (1, LANES), 1)
    kr = jnp.where(lane < MLA_ROPE, bkr_ref[...].astype(F32), 0.0)
    kr = _rope(kr, cos, sin).astype(BF16)
    for h in range(MLA_HEADS):
        lo = MLA_SLOT * h
        q_out[:, lo:lo + LANES] = q[:, lo:lo + LANES].astype(BF16)
        q_out[:, lo + LANES:lo + MLA_SLOT] = _rope(q[:, lo + LANES:lo + MLA_SLOT], cos, sin).astype(BF16)
        k_out[:, lo:lo + LANES] = kn[:, LANES * h:LANES * (h + 1)]
        k_out[:, lo + LANES:lo + MLA_SLOT] = kr


def _mla_prep(l, p, q_norm, wq_p, kv_norm, wk_p, wv_p, cos, sin):
    tm = TM_PREP
    tpb = T_TOK // tm
    full = lambda shape: pl.BlockSpec((None,) + shape, lambda i: (l,) + (0,) * len(shape))
    return pl.pallas_call(
        _mla_prep_kernel,
        grid=(ROWS // tm,),
        in_specs=[
            pl.BlockSpec((tm, MLA_Q_RANK), lambda i: (i, P_BQ // MLA_Q_RANK)),
            pl.BlockSpec((tm, MLA_KV_RANK), lambda i: (i, P_BKV // MLA_KV_RANK)),
            pl.BlockSpec((tm, LANES), lambda i: (i, P_BKR // LANES)),
            full((1, MLA_Q_RANK)),
            full((MLA_Q_RANK, MLA_HEADS * MLA_SLOT)),
            full((1, MLA_KV_RANK)),
            full((MLA_KV_RANK, MLA_HEADS * MLA_NOPE)),
            full((MLA_KV_RANK, MLA_HEADS * MLA_V)),
            pl.BlockSpec((tm, LANES), lambda i: (i % tpb, 0)),
            pl.BlockSpec((tm, LANES), lambda i: (i % tpb, 0)),
        ],
        out_specs=[
            pl.BlockSpec((tm, MLA_HEADS * MLA_SLOT), lambda i: (i, 0)),
            pl.BlockSpec((tm, MLA_HEADS * MLA_SLOT), lambda i: (i, 0)),
            pl.BlockSpec((tm, MLA_HEADS * MLA_V), lambda i: (i, 0)),
        ],
        out_shape=[
            jax.ShapeDtypeStruct((ROWS, MLA_HEADS * MLA_SLOT), BF16),
            jax.ShapeDtypeStruct((ROWS, MLA_HEADS * MLA_SLOT), BF16),
            jax.ShapeDtypeStruct((ROWS, MLA_HEADS * MLA_V), BF16),
        ],
        compiler_params=_params(("arbitrary",)),
        name="mla_prep",
    )(p, p, p, q_norm.reshape(DEPTH, 1, -1), wq_p, kv_norm.reshape(DEPTH, 1, -1), wk_p, wv_p, cos, sin)


def _mla_attn_kernel(q_ref, k_ref, v_ref, o_ref):
    def attend(n_keys):
        for h in range(MLA_HEADS):
            q = q_ref[:, MLA_SLOT * h:MLA_SLOT * (h + 1)]
            k = k_ref[0:n_keys, MLA_SLOT * h:MLA_SLOT * (h + 1)]
            s = lax.dot_general(q, k, (((1,), (1,)), ((), ())), preferred_element_type=F32)
            m = jnp.max(s, -1, keepdims=True)
            p = jnp.exp2(s - m)
            l = jnp.sum(p, -1, keepdims=True)
            o = jnp.dot(p.astype(BF16), v_ref[0:n_keys, MLA_V * h:MLA_V * (h + 1)], preferred_element_type=F32)
            o_ref[:, MLA_V * h:MLA_V * (h + 1)] = (o / l).astype(o_ref.dtype)

    @pl.when(pl.program_id(1) == 0)
    def _():
        attend(CTX_LEN)

    @pl.when(pl.program_id(1) > 0)
    def _():
        attend(T_TOK)


def _mla_attn(q, k, v):
    tq = TQ_ATT
    assert tq == CTX_LEN
    tpb = T_TOK // tq
    return pl.pallas_call(
        _mla_attn_kernel,
        grid=(BATCH, tpb),
        in_specs=[
            pl.BlockSpec((tq, MLA_HEADS * MLA_SLOT), lambda b, i: (b * tpb + i, 0)),
            pl.BlockSpec((T_TOK, MLA_HEADS * MLA_SLOT), lambda b, i: (b, 0)),
            pl.BlockSpec((T_TOK, MLA_HEADS * MLA_V), lambda b, i: (b, 0)),
        ],
        out_specs=pl.BlockSpec((tq, MLA_HEADS * MLA_V), lambda b, i: (b * tpb + i, 0)),
        out_shape=jax.ShapeDtypeStruct((ROWS, MLA_HEADS * MLA_V), BF16),
        compiler_params=_params(("arbitrary", "arbitrary")),
        name="mla_attn",
    )(q, k, v)


def _dup_half(x, upper):
    lane = lax.broadcasted_iota(jnp.int32, (1, LANES), 1)
    r = pltpu.roll(x, LANES // 2, 1)
    out = jnp.where(lane < LANES // 2, r, x) if upper else jnp.where(lane < LANES // 2, x, r)
    return out.astype(BF16)


def _swa_kernel(q_ref, k_ref, v_ref, cos_ref, sin_ref, sink_ref, o_ref, *, layer):
    tq = TQ_ATT
    band = tq + 2 * WINDOW
    qt = pl.program_id(1)
    q_row = pl.multiple_of(qt * tq, tq)
    lane = lax.broadcasted_iota(jnp.int32, (1, LANES), 1)
    low = lane < LANES // 2
    cos_q = cos_ref[pl.ds(q_row, tq), :]
    sin_q = sin_ref[pl.ds(q_row, tq), :]

    def q_block(c):
        x = q_ref[:, LANES * c:LANES * (c + 1)].astype(F32)
        return _rope(x, cos_q, sin_q) * (SWA_SCALE * LOG2E)

    def softmax_out(qm, sink, keys, vals, masks):
        ss = []
        for kk, mk in zip(keys, masks):
            s = lax.dot_general(qm, kk, (((1,), (1,)), ((), ())), preferred_element_type=F32)
            ss.append(s if mk is None else jnp.where(mk, s, -jnp.inf))
        m = jnp.maximum(functools.reduce(jnp.maximum, [jnp.max(s, -1, keepdims=True) for s in ss]), sink)
        ps = [jnp.exp2(s - m) for s in ss]
        l = functools.reduce(jnp.add, [jnp.sum(p, -1, keepdims=True) for p in ps]) + jnp.exp2(sink - m)
        o = functools.reduce(jnp.add, [jnp.dot(p.astype(BF16), vv, preferred_element_type=F32)
                                       for p, vv in zip(ps, vals)])
        return o / l

    def run(keys_of, vals_of, masks):
        for g in range(SWA_KV_HEADS):
            keys = keys_of(g)
            vals = vals_of(g)
            for c in (2 * g, 2 * g + 1):
                qc = q_block(c)
                o_lo = softmax_out(jnp.where(low, qc, 0.0).astype(BF16), sink_ref[layer, 2 * c] * LOG2E, keys, vals, masks)
                o_hi = softmax_out(jnp.where(low, 0.0, qc).astype(BF16), sink_ref[layer, 2 * c + 1] * LOG2E, keys, vals, masks)
                o_ref[:, LANES * c:LANES * (c + 1)] = jnp.where(low, o_lo, o_hi).astype(o_ref.dtype)

    def ctx_kv(ref, g):
        return _dup_half(ref[0:CTX_LEN, LANES * (g // 2):LANES * (g // 2 + 1)].astype(F32), g % 2 == 1)

    @pl.when(qt == 0)
    def _():
        run(lambda g: [ctx_kv(k_ref, g)], lambda g: [ctx_kv(v_ref, g)], [None])

    @pl.when(qt > 0)
    def _():
        start = pl.multiple_of(jnp.minimum(q_row - WINDOW, T_TOK - band), LANES)
        qpos = q_row - CTX_LEN + lax.broadcasted_iota(jnp.int32, (tq, 1), 0)
        kpos = start - CTX_LEN + lax.broadcasted_iota(jnp.int32, (1, band), 1)
        valid = (jnp.abs(kpos - qpos) <= WINDOW) & (kpos >= 0)
        cos_k = cos_ref[pl.ds(start, band), :]
        sin_k = sin_ref[pl.ds(start, band), :]

        def band_k(g):
            kb = k_ref[pl.ds(start, band), LANES * (g // 2):LANES * (g // 2 + 1)].astype(F32)
            return _dup_half(_rope(kb, cos_k, sin_k), g % 2 == 1)

        def band_v(g):
            return _dup_half(v_ref[pl.ds(start, band), LANES * (g // 2):LANES * (g // 2 + 1)].astype(F32), g % 2 == 1)

        run(lambda g: [ctx_kv(k_ref, g), band_k(g)], lambda g: [ctx_kv(v_ref, g), band_v(g)], [None, valid])


def _swa(l, p, cos, sin, sinks):
    tq = TQ_ATT
    tpb = T_TOK // tq
    kvw = SWA_KV_HEADS * SWA_HEAD_DIM
    return pl.pallas_call(
        functools.partial(_swa_kernel, layer=l),
        grid=(BATCH, tpb),
        in_specs=[
            pl.BlockSpec((tq, SWA_HEADS * SWA_HEAD_DIM), lambda b, i: (b * tpb + i, P_CQ // (SWA_HEADS * SWA_HEAD_DIM))),
            pl.BlockSpec((T_TOK, kvw), lambda b, i: (b, P_CK // kvw)),
            pl.BlockSpec((T_TOK, kvw), lambda b, i: (b, P_CV // kvw)),
            pl.BlockSpec((T_TOK, LANES), lambda b, i: (0, 0)),
            pl.BlockSpec((T_TOK, LANES), lambda b, i: (0, 0)),
            pl.BlockSpec(memory_space=pltpu.SMEM),
        ],
        out_specs=pl.BlockSpec((tq, SWA_HEADS * SWA_HEAD_DIM), lambda b, i: (b * tpb + i, 0)),
        out_shape=jax.ShapeDtypeStruct((ROWS, SWA_HEADS * SWA_HEAD_DIM), BF16),
        compiler_params=_params(("arbitrary", "arbitrary")),
        name="swa",
    )(p, p, p, cos, sin, sinks)


def _gate_rows(i, tpb, tm, gvc_ref, gvb_ref, has_ctx):
    return _row_select(i, tpb, tm, gvc_ref[0, 0], gvb_ref[0, 0]) if has_ctx else gvb_ref[0, 0]


def _merge_kernel(ya_ref, yb_ref, yc_ref, ga_ref, gb_ref, gc_ref, wb_ref, wo_ref, x_ref, gvb_ref, gvc_ref,
                  lng_ref, lnb_ref, o_ref, *, tpb, tm, nj, has_ctx):
    i = pl.program_id(0)
    j = pl.program_id(1)

    @pl.when(j == 0)
    def _():
        o_ref[...] = jnp.zeros_like(o_ref)

    z = None
    for n, (y_ref, g_ref) in enumerate(((ya_ref, ga_ref), (yb_ref, gb_ref), (yc_ref, gc_ref))):
        zn = jnp.dot(y_ref[...], wb_ref[n], preferred_element_type=F32)
        zn = jax.nn.sigmoid(g_ref[...].astype(F32)) * zn
        z = zn if z is None else z + zn
    o_ref[...] += jnp.dot(z.astype(BF16), wo_ref[...], preferred_element_type=F32)

    @pl.when(j == nj - 1)
    def _():
        gate = _gate_rows(i, tpb, tm, gvc_ref, gvb_ref, has_ctx)
        o_ref[...] = _layer_norm(ALPHA * x_ref[...] + gate * o_ref[...], lng_ref[...], lnb_ref[...])


def _merge(l, ya, yb, yc, p, wb, wo, x, ada, ln_g, ln_b, latent_only):
    tm = TM_MERGE_LATENT if latent_only else TM_MERGE
    tj = TJ_MERGE
    tpb = (SEQ if latent_only else T_TOK) // tm
    n_rows = BATCH * SEQ if latent_only else ROWS
    nj = D_MODEL // tj
    ybs = _row_spec(tm, BRANCH_W, lambda j: 0, latent_only)
    gate = lambda n: _row_spec(tm, tj, lambda j: (P_GATE + n * D_MODEL) // tj + j, latent_only)
    return pl.pallas_call(
        functools.partial(_merge_kernel, tpb=tpb, tm=tm, nj=nj, has_ctx=not latent_only),
        grid=(n_rows // tm, nj),
        in_specs=[ybs, ybs, ybs, gate(0), gate(1), gate(2),
                  pl.BlockSpec((None, None, N_BRANCH, BRANCH_W, tj), lambda i, j: (l, j, 0, 0, 0)),
                  pl.BlockSpec((None, tj, D_MODEL), lambda i, j: (l, j, 0)),
                  _row_spec(tm, D_MODEL, lambda j: 0, latent_only)]
        + _vec_specs(l, 2, tpb) + [_layer_vec(l), _layer_vec(l)],
        out_specs=pl.BlockSpec((tm, D_MODEL), lambda i, j: (i, 0)),
        out_shape=jax.ShapeDtypeStruct((n_rows, D_MODEL), F32),
        compiler_params=_params(("arbitrary", "arbitrary")),
        name="merge",
    )(ya, yb, yc, p, p, p, wb, wo, x, ada, ada, ln_g.reshape(DEPTH, 1, -1), ln_b.reshape(DEPTH, 1, -1))


def _ffn_kernel(x_ref, scb_ref, scc_ref, shb_ref, shc_ref, gvb_ref, gvc_ref, wg_ref, wu_ref, wo_ref,
                lng_ref, lnb_ref, o_ref, u_ref, *, tpb, tm, nj, has_ctx):
    i = pl.program_id(0)
    j = pl.program_id(1)

    @pl.when(j == 0)
    def _():
        sc = _gate_rows(i, tpb, tm, scc_ref, scb_ref, has_ctx)
        sh = _gate_rows(i, tpb, tm, shc_ref, shb_ref, has_ctx)
        u_ref[...] = (x_ref[...] * sc + sh).astype(BF16)
        o_ref[...] = jnp.zeros_like(o_ref)

    u = u_ref[...]
    gt = jnp.dot(u, wg_ref[...], preferred_element_type=F32)
    up = jnp.dot(u, wu_ref[...], preferred_element_type=F32)
    h = (gt * jax.nn.sigmoid(gt) * up).astype(BF16)
    o_ref[...] += jnp.dot(h, wo_ref[...], preferred_element_type=F32)

    @pl.when(j == nj - 1)
    def _():
        gate = _gate_rows(i, tpb, tm, gvc_ref, gvb_ref, has_ctx)
        o_ref[...] = _layer_norm(ALPHA * x_ref[...] + gate * o_ref[...], lng_ref[...], lnb_ref[...])


def _ffn(l, x, ada, w_in, w_out, ln_g, ln_b, latent_only):
    tm = TM_FFN_LATENT if latent_only else TM_FFN
    tj = TJ_FFN
    tpb = (SEQ if latent_only else T_TOK) // tm
    n_rows = x.shape[0]
    nj = FFN_HIDDEN // tj
    row = pl.BlockSpec((tm, D_MODEL), lambda i, j: (i, 0))
    row_in = pl.BlockSpec((tm, D_MODEL), lambda i, j: (i, 0), pipeline_mode=pl.Buffered(1))
    return pl.pallas_call(
        functools.partial(_ffn_kernel, tpb=tpb, tm=tm, nj=nj, has_ctx=not latent_only),
        grid=(n_rows // tm, nj),
        in_specs=[row_in] + _vec_specs(l, 4, tpb) + _vec_specs(l, 3, tpb) + _vec_specs(l, 5, tpb) + [
            pl.BlockSpec((None, None, D_MODEL, tj), lambda i, j: (l, j, 0, 0)),
            pl.BlockSpec((None, None, D_MODEL, tj), lambda i, j: (l, nj + j, 0, 0)),
            pl.BlockSpec((None, tj, D_MODEL), lambda i, j: (l, j, 0)),
            _layer_vec(l), _layer_vec(l)],
        out_specs=row,
        out_shape=jax.ShapeDtypeStruct((n_rows, D_MODEL), F32),
        scratch_shapes=[pltpu.VMEM((tm, D_MODEL), BF16)],
        compiler_params=_params(("arbitrary", "arbitrary")),
        name="ffn",
    )(x, ada, ada, ada, ada, ada, ada, w_in, w_in, w_out, ln_g.reshape(DEPTH, 1, -1), ln_b.reshape(DEPTH, 1, -1))


def _rope_tables(both_halves):
    rot = SWA_HEAD_DIM
    half = rot // 2
    n_rows = SEQ // GRID_W
    row = jnp.repeat(jnp.arange(n_rows, dtype=F32), GRID_W)
    colp = jnp.tile(jnp.arange(GRID_W, dtype=F32), n_rows)
    inv = ROPE_BASE ** (-jnp.arange(0, half, 2, dtype=F32) / half)
    ang_r = row[:, None] * inv
    ang_c = colp[:, None] * inv
    cos = jnp.concatenate([jnp.cos(ang_r), jnp.cos(ang_r), jnp.cos(ang_c), jnp.cos(ang_c)], -1)
    sin = jnp.concatenate([-jnp.sin(ang_r), jnp.sin(ang_r), -jnp.sin(ang_c), jnp.sin(ang_c)], -1)
    if both_halves:
        cos = jnp.concatenate([cos, cos], -1)
        sin = jnp.concatenate([sin, sin], -1)
    else:
        cos = jnp.concatenate([cos, jnp.ones_like(cos)], -1)
        sin = jnp.concatenate([sin, jnp.zeros_like(sin)], -1)
    cos = jnp.concatenate([jnp.ones((CTX_LEN, LANES), F32), cos], 0)
    sin = jnp.concatenate([jnp.zeros((CTX_LEN, LANES), F32), sin], 0)
    return cos, sin


def _tile_ffn_in(w_ffn_in, tj):
    nt = 2 * FFN_HIDDEN // tj
    return w_ffn_in.reshape(DEPTH, D_MODEL, nt, tj).transpose(0, 2, 1, 3).astype(BF16)


def _tile_branch(w_branch, tj):
    nj = D_MODEL // tj
    return w_branch.reshape(DEPTH, N_BRANCH, BRANCH_W, nj, tj).transpose(0, 3, 1, 2, 4).astype(BF16)


def _mla_weights(w_q_up, w_kv_up):
    wq = w_q_up.reshape(DEPTH, MLA_Q_RANK, MLA_HEADS, MLA_NOPE + MLA_ROPE)
    wq = jnp.pad(wq, ((0, 0), (0, 0), (0, 0), (0, MLA_SLOT - MLA_NOPE - MLA_ROPE)))
    wkv = w_kv_up.reshape(DEPTH, MLA_KV_RANK, MLA_HEADS, MLA_NOPE + MLA_V)
    wk = wkv[..., :MLA_NOPE].reshape(DEPTH, MLA_KV_RANK, MLA_HEADS * MLA_NOPE)
    wv = wkv[..., MLA_NOPE:].reshape(DEPTH, MLA_KV_RANK, MLA_HEADS * MLA_V)
    return wq.reshape(DEPTH, MLA_Q_RANK, MLA_HEADS * MLA_SLOT).astype(BF16), wk.astype(BF16), wv.astype(BF16)


def kernel(x, c, ctx, c_ctx, w_ada, b_ada, w_in, conv_w, conv_b, lru_wr, lru_br, lru_wi, lru_bi, lru_lambda, mla_q_norm, mla_w_q_up, mla_kv_norm, mla_w_kv_up, swa_sinks, w_branch, w_out, ln1_g, ln1_b, w_ffn_in, w_ffn_out, ln2_g, ln2_b):
    cos_mla, sin_mla = _rope_tables(False)
    cos_swa, sin_swa = _rope_tables(True)
    cvec = jnp.concatenate([c, c_ctx[None], jnp.zeros((SUBLANES - BATCH - 1, D_MODEL), F32)], 0)
    ada = _ada(cvec, w_ada, b_ada).reshape(DEPTH, 6, SUBLANES, 1, D_MODEL)
    w_p = jnp.transpose(w_in, (0, 2, 1)).astype(BF16)
    wr, wi = lru_wr.astype(BF16), lru_wi.astype(BF16)
    wq_p, wk_p, wv_p = _mla_weights(mla_w_q_up, mla_w_kv_up)
    wb, wo = _tile_branch(w_branch, TJ_MERGE), w_out.astype(BF16)
    wf_in, wf_out = _tile_ffn_in(w_ffn_in, TJ_FFN), w_ffn_out.astype(BF16)
    xs = jnp.concatenate([ctx, x], 1).reshape(ROWS, D_MODEL)
    for l in range(DEPTH):
        last = l == DEPTH - 1
        p = _mixin(l, xs, ada, w_p)
        ya = _lru(l, p, conv_w, conv_b, wr, lru_br, wi, lru_bi, lru_lambda)
        q, k, v = _mla_prep(l, p, mla_q_norm, wq_p, mla_kv_norm, wk_p, wv_p, cos_mla, sin_mla)
        yb = _mla_attn(q, k, v)
        yc = _swa(l, p, cos_swa, sin_swa, swa_sinks)
        x1 = _merge(l, ya, yb, yc, p, wb, wo, xs, ada, ln1_g, ln1_b, last)
        xs = _ffn(l, x1, ada, wf_in, wf_out, ln2_g, ln2_b, last)
    return xs.reshape(BATCH, SEQ, D_MODEL)
```

```python
import functools
import math

import jax
import jax.numpy as jnp
from jax import lax
from jax.experimental import pallas as pl
from jax.experimental.pallas import tpu as pltpu

F32 = jnp.float32
BF16 = jnp.bfloat16

D_MODEL = 2048
BATCH = 4
SEQ = 2048
DEPTH = 2
GRID_W = 64
CTX_LEN = 256
LRU_WIDTH = 1024
LRU_BLOCKS = 8
LRU_BLOCK_W = LRU_WIDTH // LRU_BLOCKS
LRU_C = 8.0
CONV_W = 4
MLA_HEADS = 8
MLA_Q_RANK = 512
MLA_KV_RANK = 256
MLA_NOPE = 128
MLA_ROPE = 64
MLA_V = 128
MLA_SCALE = (MLA_NOPE + MLA_ROPE) ** -0.5
SWA_HEADS = 16
SWA_KV_HEADS = 4
SWA_HEAD_DIM = 64
SWA_REP = SWA_HEADS // SWA_KV_HEADS
SWA_SCALE = SWA_HEAD_DIM ** -0.5
WINDOW = 128
N_BRANCH = 3
BRANCH_W = 1024
MIX_IN = 4416
FFN_HIDDEN = -(-8 * D_MODEL // (3 * 256)) * 256
ROPE_BASE = 10000.0
LN_EPS = 1e-5
RMS_EPS = 1e-6
ALPHA = (2 * DEPTH) ** 0.25

T_TOK = CTX_LEN + SEQ
ROWS = BATCH * T_TOK
LANES = 128
SUBLANES = 8
MLA_SLOT = 2 * LANES

P_AX, P_AG, P_BQ, P_BKV, P_BKR = 0, 1024, 2048, 2560, 2816
P_CQ, P_CK, P_CV, P_GATE = 3072, 4096, 4352, 4608
P_COLS = P_GATE + N_BRANCH * D_MODEL

VMEM_LIMIT = 56 * 1024 * 1024

TM_MIX = 1152
TN_MIX = 1536
TM_MERGE = 576
TJ_MERGE = 512
TM_FFN = 768
TJ_FFN = 512
TM_PREP = 576
TQ_ATT = 256
TQ_MLA = 576
TN_ADA = 2048
TM_MERGE_LATENT = 512
TM_FFN_LATENT = 1024
LOG2E = math.log2(math.e)


def _params(sem):
    return pltpu.CompilerParams(dimension_semantics=sem, vmem_limit_bytes=VMEM_LIMIT)


def _row_select(i, tiles_per_batch, tm, ctx_vec, batch_vec):
    rows = lax.broadcasted_iota(jnp.int32, (tm, 1), 0)
    n_ctx = jnp.where(i % tiles_per_batch == 0, CTX_LEN, 0)
    return jnp.where(rows < n_ctx, ctx_vec, batch_vec)


def _layer_norm(v, g, b):
    mu = jnp.mean(v, -1, keepdims=True)
    d = v - mu
    var = jnp.mean(d * d, -1, keepdims=True)
    return d * lax.rsqrt(var + LN_EPS) * g + b


def _rms_norm(v, g):
    return v * lax.rsqrt(jnp.mean(v * v, -1, keepdims=True) + RMS_EPS) * g


def _swap_pairs(x):
    lane = lax.broadcasted_iota(jnp.int32, (1, LANES), 1)
    first = (lane & 16) == 0
    return jnp.where(first, pltpu.roll(x, LANES - 16, 1), pltpu.roll(x, 16, 1))


def _rope(x, cos, sin):
    return x * cos + _swap_pairs(x) * sin


def _ada_kernel(c_ref, w_ref, b_ref, o_ref, *, chunks_per_vec):
    j = pl.program_id(1)
    c = c_ref[...]
    a = (c * jax.nn.sigmoid(c)).astype(BF16)
    acc = jnp.dot(a, w_ref[0].astype(BF16), preferred_element_type=F32)
    vec = j // chunks_per_vec
    one = jnp.where((vec == 1) | (vec == 4), 1.0, 0.0)
    o_ref[0, 0] = acc + b_ref[0] + one


def _ada(cvec, w_ada, b_ada):
    tn = TN_ADA
    cpv = D_MODEL // tn
    return pl.pallas_call(
        functools.partial(_ada_kernel, chunks_per_vec=cpv),
        grid=(DEPTH, 6 * cpv),
        in_specs=[
            pl.BlockSpec((SUBLANES, D_MODEL), lambda l, j: (0, 0)),
            pl.BlockSpec((1, D_MODEL, tn), lambda l, j: (l, 0, j)),
            pl.BlockSpec((1, 1, tn), lambda l, j: (l, 0, j)),
        ],
        out_specs=pl.BlockSpec((1, 1, SUBLANES, tn), lambda l, j: (l, j // cpv, 0, j % cpv)),
        out_shape=jax.ShapeDtypeStruct((DEPTH, 6, SUBLANES, D_MODEL), F32),
        compiler_params=_params(("arbitrary", "arbitrary")),
        name="ada",
    )(cvec, w_ada, b_ada.reshape(DEPTH, 1, 6 * D_MODEL))


def _vec_specs(l, vec, tiles_per_batch):
    return [
        pl.BlockSpec((None, 1, 1, 1, D_MODEL), lambda i, j: (l, vec, i // tiles_per_batch, 0, 0)),
        pl.BlockSpec((None, 1, 1, 1, D_MODEL), lambda i, j: (l, vec, BATCH, 0, 0)),
    ]


def _layer_vec(l):
    return pl.BlockSpec((None, 1, D_MODEL), lambda i, j: (l, 0, 0))


def _row_spec(tm, width, col_block, latent_only):
    if not latent_only:
        return pl.BlockSpec((tm, width), lambda i, j: (i, col_block(j)))
    tiles = SEQ // tm

    def index(i, j):
        row = (i // tiles) * T_TOK + CTX_LEN + (i % tiles) * tm
        return pl.multiple_of(row, CTX_LEN), pl.multiple_of(col_block(j) * width, LANES)

    return pl.BlockSpec((pl.Element(tm), pl.Element(width)), index)


def _mixin_kernel(x_ref, scb_ref, scc_ref, shb_ref, shc_ref, w_ref, o_ref, u_ref, *, tpb, tm):
    i = pl.program_id(0)

    @pl.when(pl.program_id(1) == 0)
    def _():
        sc = _row_select(i, tpb, tm, scc_ref[0, 0], scb_ref[0, 0])
        sh = _row_select(i, tpb, tm, shc_ref[0, 0], shb_ref[0, 0])
        u_ref[...] = (x_ref[...] * sc + sh).astype(BF16)

    o_ref[...] = lax.dot_general(u_ref[...], w_ref[...], (((1,), (1,)), ((), ())),
                                 preferred_element_type=F32).astype(o_ref.dtype)


def _mixin(l, x, ada, w_t):
    tm, tn = TM_MIX, TN_MIX
    tpb = T_TOK // tm
    pad = P_CQ - (P_BKR + MLA_ROPE)
    assert P_CQ % tn == 0 and w_t.shape[1] + pad == P_COLS

    def w_index(i, j):
        row = j * tn - jnp.where(j >= P_CQ // tn, pad, 0)
        return l, pl.multiple_of(row, MLA_ROPE), 0

    return pl.pallas_call(
        functools.partial(_mixin_kernel, tpb=tpb, tm=tm),
        grid=(ROWS // tm, P_COLS // tn),
        in_specs=[pl.BlockSpec((tm, D_MODEL), lambda i, j: (i, 0))]
        + _vec_specs(l, 1, tpb)
        + _vec_specs(l, 0, tpb)
        + [pl.BlockSpec((None, pl.Element(tn), pl.Element(D_MODEL)), w_index)],
        out_specs=pl.BlockSpec((tm, tn), lambda i, j: (i, j)),
        out_shape=jax.ShapeDtypeStruct((ROWS, P_COLS), BF16),
        scratch_shapes=[pltpu.VMEM((tm, D_MODEL), BF16)],
        compiler_params=_params(("arbitrary", "arbitrary")),
        name="mix_in",
    )(x, ada, ada, ada, ada, w_t)


LRU_SEG_FWD = 292
LRU_SEG_BWD = 260
LRU_PAD_TOK = SUBLANES * LRU_SEG_FWD
assert LRU_PAD_TOK >= T_TOK and CTX_LEN + SUBLANES * LRU_SEG_BWD == LRU_PAD_TOK
assert LRU_SEG_FWD % 8 == 4 and LRU_SEG_BWD % 8 == 4


def _lru_kernel(ax_ref, ag_ref, cw_ref, cb_ref, wr_ref, br_ref, wi_ref, bi_ref, lam_ref, y_ref,
                a_s, b_s, h_s, p_s):
    n_tok = T_TOK
    x = ax_ref[...].astype(F32)
    t = lax.broadcasted_iota(jnp.int32, (n_tok, 1), 0)
    is_ctx = t < CTX_LEN
    t_loc = jnp.where(is_ctx, t, t - CTX_LEN)
    seg_len = jnp.where(is_ctx, CTX_LEN, SEQ)
    xm1 = jnp.where(t_loc >= 1, pltpu.roll(x, 1, 0), 0.0)
    xm2 = jnp.where(t_loc >= 2, pltpu.roll(x, 2, 0), 0.0)
    xp1 = jnp.where(t_loc <= seg_len - 2, pltpu.roll(x, n_tok - 1, 0), 0.0)
    w = cw_ref[...]
    xc = xm2 * w[0:1] + xm1 * w[1:2] + x * w[2:3] + xp1 * w[3:4] + cb_ref[...]
    xb = xc.astype(BF16)

    for d in range(2):
        r = jax.nn.sigmoid(jnp.dot(xb, wr_ref[d, 0], preferred_element_type=F32) + br_ref[d])
        ig = jax.nn.sigmoid(jnp.dot(xb, wi_ref[d, 0], preferred_element_type=F32) + bi_ref[d])
        z = -lam_ref[d]
        softplus = jnp.maximum(z, 0.0) + jnp.log1p(jnp.exp(-jnp.abs(z)))
        log_a = (-LRU_C) * r * softplus
        a = jnp.exp(log_a)
        v = jnp.tanh(-log_a) * (a * a + 1.0)
        mult = jnp.where(v > 0.0, v * lax.rsqrt(v), 0.0)
        a_s[d, 0:n_tok, :] = a
        b_s[d, 0:n_tok, :] = mult * ig * xc
        a_s[d, n_tok:LRU_PAD_TOK, :] = jnp.ones((LRU_PAD_TOK - n_tok, LANES), F32)
        b_s[d, n_tok:LRU_PAD_TOK, :] = jnp.zeros((LRU_PAD_TOK - n_tok, LANES), F32)

    a = a_s[1, 0:CTX_LEN, :]
    bb = b_s[1, 0:CTX_LEN, :]
    sub = lax.broadcasted_iota(jnp.int32, (CTX_LEN, 1), 0) & (SUBLANES - 1)
    for s in (1, 2, 4):
        m = sub < SUBLANES - s
        a_sh = pltpu.roll(a, CTX_LEN - s, 0)
        b_sh = pltpu.roll(bb, CTX_LEN - s, 0)
        bb = jnp.where(m, bb + a * b_sh, bb)
        a = jnp.where(m, a * a_sh, a)
    a_s[1, 0:CTX_LEN, :] = a
    b_s[1, 0:CTX_LEN, :] = bb

    def ctx_bwd(k, h):
        rows = pl.ds(pl.multiple_of((CTX_LEN // SUBLANES - 1 - k) * SUBLANES, SUBLANES), SUBLANES)
        hg = b_s[1, rows, :] + a_s[1, rows, :] * h
        b_s[1, rows, :] = hg
        return hg[0:1]

    h_ctx0 = lax.fori_loop(0, CTX_LEN // SUBLANES, ctx_bwd, jnp.zeros((1, LANES), F32), unroll=4)

    def seg_rows(base, s, seg):
        return pl.ds(base + s, SUBLANES, stride=seg)

    def fwd_step(s, h, p):
        rows = seg_rows(0, s, LRU_SEG_FWD)
        av = a_s[0, rows, :]
        h = av * h + b_s[0, rows, :]
        p = av * p
        h_s[0, rows, :] = h
        p_s[0, rows, :] = p
        return h, p

    def bwd_step(s, h, p):
        rows = seg_rows(CTX_LEN, LRU_SEG_BWD - 1 - s, LRU_SEG_BWD)
        av = a_s[1, rows, :]
        h = av * h + b_s[1, rows, :]
        p = av * p
        h_s[1, rows, :] = h
        p_s[1, rows, :] = p
        return h, p

    zeros8 = jnp.zeros((SUBLANES, LANES), F32)
    ones8 = jnp.ones((SUBLANES, LANES), F32)

    def both(s, c):
        return fwd_step(s, c[0], c[1]) + bwd_step(s, c[2], c[3])

    hf, pf, hb, pb = lax.fori_loop(0, LRU_SEG_BWD, both, (zeros8, ones8, zeros8, ones8), unroll=4)
    hf, pf = lax.fori_loop(LRU_SEG_BWD, LRU_SEG_FWD, lambda s, c: fwd_step(s, c[0], c[1]), (hf, pf), unroll=4)

    seg = lax.broadcasted_iota(jnp.int32, (SUBLANES, 1), 0)
    cf = zeros8
    cb = jnp.where(seg == SUBLANES - 1, h_ctx0, 0.0)
    for _ in range(SUBLANES - 1):
        cf = jnp.where(seg >= 1, pltpu.roll(hf + pf * cf, 1, 0), 0.0)
        cb = jnp.where(seg <= SUBLANES - 2, pltpu.roll(hb + pb * cb, SUBLANES - 1, 0), h_ctx0)

    def fix_fwd(s):
        rows = seg_rows(0, s, LRU_SEG_FWD)
        b_s[0, rows, :] = h_s[0, rows, :] + p_s[0, rows, :] * cf

    def fix_bwd(s):
        rows = seg_rows(CTX_LEN, s, LRU_SEG_BWD)
        b_s[1, rows, :] = h_s[1, rows, :] + p_s[1, rows, :] * cb

    def fix_both(s, c):
        fix_fwd(s)
        fix_bwd(s)
        return c

    def fix_tail(s, c):
        fix_fwd(s)
        return c

    lax.fori_loop(0, LRU_SEG_BWD, fix_both, 0, unroll=4)
    lax.fori_loop(LRU_SEG_BWD, LRU_SEG_FWD, fix_tail, 0, unroll=4)

    g = ag_ref[...].astype(F32)
    cdf = 0.5 * (1.0 + jnp.tanh(math.sqrt(2.0 / math.pi) * (g + 0.044715 * (g * g * g))))
    h_sum = b_s[0, 0:n_tok, :] + b_s[1, 0:n_tok, :]
    y_ref[...] = (h_sum * (g * cdf)).astype(y_ref.dtype)


def _lru(l, p, conv_w, conv_b, wr, br, wi, bi, lam):
    nb = LRU_BLOCKS
    col = lambda off: (lambda b, n: (b, off // LANES + n))
    vec2 = pl.BlockSpec((None, 2, 1, LANES), lambda b, n: (l, 0, 0, n))
    mat = pl.BlockSpec((None, 2, 1, LRU_BLOCK_W, LRU_BLOCK_W), lambda b, n: (l, 0, n, 0, 0))
    return pl.pallas_call(
        _lru_kernel,
        grid=(BATCH, nb),
        in_specs=[
            pl.BlockSpec((T_TOK, LANES), col(P_AX)),
            pl.BlockSpec((T_TOK, LANES), col(P_AG)),
            pl.BlockSpec((None, CONV_W, LANES), lambda b, n: (l, 0, n)),
            pl.BlockSpec((None, 1, LANES), lambda b, n: (l, 0, n)),
            mat, vec2, mat, vec2, vec2,
        ],
        out_specs=pl.BlockSpec((T_TOK, LANES), lambda b, n: (b, n)),
        out_shape=jax.ShapeDtypeStruct((ROWS, LRU_WIDTH), BF16),
        scratch_shapes=[
            pltpu.VMEM((2, LRU_PAD_TOK, LANES), F32),
            pltpu.VMEM((2, LRU_PAD_TOK, LANES), F32),
            pltpu.VMEM((2, LRU_PAD_TOK, LANES), F32),
            pltpu.VMEM((2, LRU_PAD_TOK, LANES), F32),
        ],
        compiler_params=_params(("arbitrary", "arbitrary")),
        name="rg_lru",
    )(p, p, conv_w, conv_b.reshape(DEPTH, 1, LRU_WIDTH), wr, br.reshape(DEPTH, 2, 1, LRU_WIDTH), wi,
      bi.reshape(DEPTH, 2, 1, LRU_WIDTH), lam.reshape(DEPTH, 2, 1, LRU_WIDTH))


def _mla_prep_kernel(bq_ref, bkv_ref, bkr_ref, qn_ref, wq_ref, kvn_ref, wk_ref, wv_ref, cos_ref, sin_ref,
                     q_out, k_out, v_out):
    cos = cos_ref[...]
    sin = sin_ref[...]
    qn = _rms_norm(bq_ref[...].astype(F32), qn_ref[...])
    q = jnp.dot(qn.astype(BF16), wq_ref[...], preferred_element_type=F32) * (MLA_SCALE * LOG2E)
    kvn = _rms_norm(bkv_ref[...].astype(F32), kvn_ref[...]).astype(BF16)
    kn = jnp.dot(kvn, wk_ref[...], preferred_element_type=F32).astype(BF16)
    v_out[...] = jnp.dot(kvn, wv_ref[...], preferred_element_type=F32).astype(BF16)
    lane = lax.broadcasted_iota(jnp.int32, (1, LANES), 1)
    kr = jnp.where(lane < MLA_ROPE, bkr_ref[...].astype(F32), 0.0)
    kr = _rope(kr, cos, sin).astype(BF16)
    for h in range(MLA_HEADS):
        lo = MLA_SLOT * h
        q_out[:, lo:lo + LANES] = q[:, lo:lo + LANES].astype(BF16)
        q_out[:, lo + LANES:lo + MLA_SLOT] = _rope(q[:, lo + LANES:lo + MLA_SLOT], cos, sin).astype(BF16)
        k_out[:, lo:lo + LANES] = kn[:, LANES * h:LANES * (h + 1)]
        k_out[:, lo + LANES:lo + MLA_SLOT] = kr


def _mla_prep(l, p, q_norm, wq_p, kv_norm, wk_p, wv_p, cos, sin):
    tm = TM_PREP
    tpb = T_TOK // tm
    full = lambda shape: pl.BlockSpec((None,) + shape, lambda i: (l,) + (0,) * len(shape))
    return pl.pallas_call(
        _mla_prep_kernel,
        grid=(ROWS // tm,),
        in_specs=[
            pl.BlockSpec((tm, MLA_Q_RANK), lambda i: (i, P_BQ // MLA_Q_RANK)),
            pl.BlockSpec((tm, MLA_KV_RANK), lambda i: (i, P_BKV // MLA_KV_RANK)),
            pl.BlockSpec((tm, LANES), lambda i: (i, P_BKR // LANES)),
            full((1, MLA_Q_RANK)),
            full((MLA_Q_RANK, MLA_HEADS * MLA_SLOT)),
            full((1, MLA_KV_RANK)),
            full((MLA_KV_RANK, MLA_HEADS * MLA_NOPE)),
            full((MLA_KV_RANK, MLA_HEADS * MLA_V)),
            pl.BlockSpec((tm, LANES), lambda i: (i % tpb, 0)),
            pl.BlockSpec((tm, LANES), lambda i: (i % tpb, 0)),
        ],
        out_specs=[
            pl.BlockSpec((tm, MLA_HEADS * MLA_SLOT), lambda i: (i, 0)),
            pl.BlockSpec((tm, MLA_HEADS * MLA_SLOT), lambda i: (i, 0)),
            pl.BlockSpec((tm, MLA_HEADS * MLA_V), lambda i: (i, 0)),
        ],
        out_shape=[
            jax.ShapeDtypeStruct((ROWS, MLA_HEADS * MLA_SLOT), BF16),
            jax.ShapeDtypeStruct((ROWS, MLA_HEADS * MLA_SLOT), BF16),
            jax.ShapeDtypeStruct((ROWS, MLA_HEADS * MLA_V), BF16),
        ],
        compiler_params=_params(("arbitrary",)),
        name="mla_prep",
    )(p, p, p, q_norm.reshape(DEPTH, 1, -1), wq_p, kv_norm.reshape(DEPTH, 1, -1), wk_p, wv_p, cos, sin)


def _mla_attn_kernel(q_ref, k_ref, v_ref, o_ref):
    tq = q_ref.shape[0]

    def attend(row0, n_rows, n_keys):
        for h in range(MLA_HEADS):
            q = q_ref[row0:row0 + n_rows, MLA_SLOT * h:MLA_SLOT * (h + 1)]
            k = k_ref[0:n_keys, MLA_SLOT * h:MLA_SLOT * (h + 1)]
            s = lax.dot_general(q, k, (((1,), (1,)), ((), ())), preferred_element_type=F32)
            m = jnp.max(s, -1, keepdims=True)
            p = jnp.exp2(s - m)
            l = jnp.sum(p, -1, keepdims=True)
            o = jnp.dot(p.astype(BF16), v_ref[0:n_keys, MLA_V * h:MLA_V * (h + 1)], preferred_element_type=F32)
            o_ref[row0:row0 + n_rows, MLA_V * h:MLA_V * (h + 1)] = (o / l).astype(o_ref.dtype)

    @pl.when(pl.program_id(1) == 0)
    def _():
        attend(0, CTX_LEN, CTX_LEN)
        attend(CTX_LEN, tq - CTX_LEN, T_TOK)

    @pl.when(pl.program_id(1) > 0)
    def _():
        attend(0, tq, T_TOK)


def _mla_attn(q, k, v):
    tq = TQ_MLA
    assert tq > CTX_LEN and T_TOK % tq == 0 and tq % (2 * SUBLANES) == 0
    tpb = T_TOK // tq
    per_batch = lambda width: pl.BlockSpec((T_TOK, width), lambda b, i: (b, 0), pipeline_mode=pl.Buffered(1))
    return pl.pallas_call(
        _mla_attn_kernel,
        grid=(BATCH, tpb),
        in_specs=[
            pl.BlockSpec((tq, MLA_HEADS * MLA_SLOT), lambda b, i: (b * tpb + i, 0)),
            per_batch(MLA_HEADS * MLA_SLOT),
            per_batch(MLA_HEADS * MLA_V),
        ],
        out_specs=pl.BlockSpec((tq, MLA_HEADS * MLA_V), lambda b, i: (b * tpb + i, 0)),
        out_shape=jax.ShapeDtypeStruct((ROWS, MLA_HEADS * MLA_V), BF16),
        compiler_params=_params(("arbitrary", "arbitrary")),
        name="mla_attn",
    )(q, k, v)


def _dup_half(x, upper):
    lane = lax.broadcasted_iota(jnp.int32, (1, LANES), 1)
    r = pltpu.roll(x, LANES // 2, 1)
    out = jnp.where(lane < LANES // 2, r, x) if upper else jnp.where(lane < LANES // 2, x, r)
    return out.astype(BF16)


def _swa_kernel(q_ref, k_ref, v_ref, cos_ref, sin_ref, sink_ref, o_ref, *, layer):
    tq = TQ_ATT
    band = tq + 2 * WINDOW
    qt = pl.program_id(1)
    q_row = pl.multiple_of(qt * tq, tq)
    lane = lax.broadcasted_iota(jnp.int32, (1, LANES), 1)
    low = lane < LANES // 2
    cos_q = cos_ref[pl.ds(q_row, tq), :]
    sin_q = sin_ref[pl.ds(q_row, tq), :]

    def q_block(c):
        x = q_ref[:, LANES * c:LANES * (c + 1)].astype(F32)
        return _rope(x, cos_q, sin_q) * (SWA_SCALE * LOG2E)

    def softmax_out(qm, sink, keys, vals, masks):
        ss = []
        for kk, mk in zip(keys, masks):
            s = lax.dot_general(qm, kk, (((1,), (1,)), ((), ())), preferred_element_type=F32)
            ss.append(s if mk is None else jnp.where(mk, s, -jnp.inf))
        m = jnp.maximum(functools.reduce(jnp.maximum, [jnp.max(s, -1, keepdims=True) for s in ss]), sink)
        ps = [jnp.exp2(s - m) for s in ss]
        l = functools.reduce(jnp.add, [jnp.sum(p, -1, keepdims=True) for p in ps]) + jnp.exp2(sink - m)
        o = functools.reduce(jnp.add, [jnp.dot(p.astype(BF16), vv, preferred_element_type=F32)
                                       for p, vv in zip(ps, vals)])
        return o / l

    def run(keys_of, vals_of, masks):
        for g in range(SWA_KV_HEADS):
            keys = keys_of(g)
            vals = vals_of(g)
            for c in (2 * g, 2 * g + 1):
                qc = q_block(c)
                o_lo = softmax_out(jnp.where(low, qc, 0.0).astype(BF16), sink_ref[layer, 2 * c] * LOG2E, keys, vals, masks)
                o_hi = softmax_out(jnp.where(low, 0.0, qc).astype(BF16), sink_ref[layer, 2 * c + 1] * LOG2E, keys, vals, masks)
                o_ref[:, LANES * c:LANES * (c + 1)] = jnp.where(low, o_lo, o_hi).astype(o_ref.dtype)

    def ctx_kv(ref, g):
        return _dup_half(ref[0:CTX_LEN, LANES * (g // 2):LANES * (g // 2 + 1)].astype(F32), g % 2 == 1)

    @pl.when(qt == 0)
    def _():
        run(lambda g: [ctx_kv(k_ref, g)], lambda g: [ctx_kv(v_ref, g)], [None])

    @pl.when(qt > 0)
    def _():
        start = pl.multiple_of(jnp.minimum(q_row - WINDOW, T_TOK - band), LANES)
        qpos = q_row - CTX_LEN + lax.broadcasted_iota(jnp.int32, (tq, 1), 0)
        kpos = start - CTX_LEN + lax.broadcasted_iota(jnp.int32, (1, band), 1)
        valid = (jnp.abs(kpos - qpos) <= WINDOW) & (kpos >= 0)
        cos_k = cos_ref[pl.ds(start, band), :]
        sin_k = sin_ref[pl.ds(start, band), :]

        def band_k(g):
            kb = k_ref[pl.ds(start, band), LANES * (g // 2):LANES * (g // 2 + 1)].astype(F32)
            return _dup_half(_rope(kb, cos_k, sin_k), g % 2 == 1)

        def band_v(g):
            return _dup_half(v_ref[pl.ds(start, band), LANES * (g // 2):LANES * (g // 2 + 1)].astype(F32), g % 2 == 1)

        run(lambda g: [ctx_kv(k_ref, g), band_k(g)], lambda g: [ctx_kv(v_ref, g), band_v(g)], [None, valid])


def _swa(l, p, cos, sin, sinks):
    tq = TQ_ATT
    tpb = T_TOK // tq
    kvw = SWA_KV_HEADS * SWA_HEAD_DIM
    return pl.pallas_call(
        functools.partial(_swa_kernel, layer=l),
        grid=(BATCH, tpb),
        in_specs=[
            pl.BlockSpec((tq, SWA_HEADS * SWA_HEAD_DIM), lambda b, i: (b * tpb + i, P_CQ // (SWA_HEADS * SWA_HEAD_DIM))),
            pl.BlockSpec((T_TOK, kvw), lambda b, i: (b, P_CK // kvw)),
            pl.BlockSpec((T_TOK, kvw), lambda b, i: (b, P_CV // kvw)),
            pl.BlockSpec((T_TOK, LANES), lambda b, i: (0, 0)),
            pl.BlockSpec((T_TOK, LANES), lambda b, i: (0, 0)),
            pl.BlockSpec(memory_space=pltpu.SMEM),
        ],
        out_specs=pl.BlockSpec((tq, SWA_HEADS * SWA_HEAD_DIM), lambda b, i: (b * tpb + i, 0)),
        out_shape=jax.ShapeDtypeStruct((ROWS, SWA_HEADS * SWA_HEAD_DIM), BF16),
        compiler_params=_params(("arbitrary", "arbitrary")),
        name="swa",
    )(p, p, p, cos, sin, sinks)


def _gate_rows(i, tpb, tm, gvc_ref, gvb_ref, has_ctx):
    return _row_select(i, tpb, tm, gvc_ref[0, 0], gvb_ref[0, 0]) if has_ctx else gvb_ref[0, 0]


def _merge_kernel(ya_ref, yb_ref, yc_ref, ga_ref, gb_ref, gc_ref, wb_ref, wo_ref, x_ref, gvb_ref, gvc_ref,
                  lng_ref, lnb_ref, o_ref, *, tpb, tm, nj, has_ctx):
    i = pl.program_id(0)
    j = pl.program_id(1)

    @pl.when(j == 0)
    def _():
        o_ref[...] = jnp.zeros_like(o_ref)

    z = None
    for n, (y_ref, g_ref) in enumerate(((ya_ref, ga_ref), (yb_ref, gb_ref), (yc_ref, gc_ref))):
        zn = jnp.dot(y_ref[...], wb_ref[n], preferred_element_type=F32)
        zn = jax.nn.sigmoid(g_ref[...].astype(F32)) * zn
        z = zn if z is None else z + zn
    o_ref[...] += jnp.dot(z.astype(BF16), wo_ref[...], preferred_element_type=F32)

    @pl.when(j == nj - 1)
    def _():
        gate = _gate_rows(i, tpb, tm, gvc_ref, gvb_ref, has_ctx)
        o_ref[...] = _layer_norm(ALPHA * x_ref[...] + gate * o_ref[...], lng_ref[...], lnb_ref[...])


def _merge(l, ya, yb, yc, p, wb, wo, x, ada, ln_g, ln_b, latent_only):
    tm = TM_MERGE_LATENT if latent_only else TM_MERGE
    tj = TJ_MERGE
    tpb = (SEQ if latent_only else T_TOK) // tm
    n_rows = BATCH * SEQ if latent_only else ROWS
    nj = D_MODEL // tj
    ybs = _row_spec(tm, BRANCH_W, lambda j: 0, latent_only)
    gate = lambda n: _row_spec(tm, tj, lambda j: (P_GATE + n * D_MODEL) // tj + j, latent_only)
    return pl.pallas_call(
        functools.partial(_merge_kernel, tpb=tpb, tm=tm, nj=nj, has_ctx=not latent_only),
        grid=(n_rows // tm, nj),
        in_specs=[ybs, ybs, ybs, gate(0), gate(1), gate(2),
                  pl.BlockSpec((None, N_BRANCH, BRANCH_W, tj), lambda i, j: (l, 0, 0, j)),
                  pl.BlockSpec((None, tj, D_MODEL), lambda i, j: (l, j, 0)),
                  _row_spec(tm, D_MODEL, lambda j: 0, latent_only)]
        + _vec_specs(l, 2, tpb) + [_layer_vec(l), _layer_vec(l)],
        out_specs=pl.BlockSpec((tm, D_MODEL), lambda i, j: (i, 0)),
        out_shape=jax.ShapeDtypeStruct((n_rows, D_MODEL), F32),
        compiler_params=_params(("arbitrary", "arbitrary")),
        name="merge",
    )(ya, yb, yc, p, p, p, wb, wo, x, ada, ada, ln_g.reshape(DEPTH, 1, -1), ln_b.reshape(DEPTH, 1, -1))


def _ffn_kernel(x_ref, scb_ref, scc_ref, shb_ref, shc_ref, gvb_ref, gvc_ref, wg_ref, wu_ref, wo_ref,
                lng_ref, lnb_ref, o_ref, u_ref, *, tpb, tm, nj, has_ctx):
    i = pl.program_id(0)
    j = pl.program_id(1)

    @pl.when(j == 0)
    def _():
        sc = _gate_rows(i, tpb, tm, scc_ref, scb_ref, has_ctx)
        sh = _gate_rows(i, tpb, tm, shc_ref, shb_ref, has_ctx)
        u_ref[...] = (x_ref[...] * sc + sh).astype(BF16)
        o_ref[...] = jnp.zeros_like(o_ref)

    u = u_ref[...]
    gt = jnp.dot(u, wg_ref[...], preferred_element_type=F32)
    up = jnp.dot(u, wu_ref[...], preferred_element_type=F32)
    h = (gt * jax.nn.sigmoid(gt) * up).astype(BF16)
    o_ref[...] += jnp.dot(h, wo_ref[...], preferred_element_type=F32)

    @pl.when(j == nj - 1)
    def _():
        gate = _gate_rows(i, tpb, tm, gvc_ref, gvb_ref, has_ctx)
        o_ref[...] = _layer_norm(ALPHA * x_ref[...] + gate * o_ref[...], lng_ref[...], lnb_ref[...])


def _ffn(l, x, ada, w_in, w_out, ln_g, ln_b, latent_only):
    tm = TM_FFN_LATENT if latent_only else TM_FFN
    tj = TJ_FFN
    tpb = (SEQ if latent_only else T_TOK) // tm
    n_rows = x.shape[0]
    nj = FFN_HIDDEN // tj
    row = pl.BlockSpec((tm, D_MODEL), lambda i, j: (i, 0))
    row_in = pl.BlockSpec((tm, D_MODEL), lambda i, j: (i, 0), pipeline_mode=pl.Buffered(1))
    return pl.pallas_call(
        functools.partial(_ffn_kernel, tpb=tpb, tm=tm, nj=nj, has_ctx=not latent_only),
        grid=(n_rows // tm, nj),
        in_specs=[row_in] + _vec_specs(l, 4, tpb) + _vec_specs(l, 3, tpb) + _vec_specs(l, 5, tpb) + [
            pl.BlockSpec((None, D_MODEL, tj), lambda i, j: (l, 0, j)),
            pl.BlockSpec((None, D_MODEL, tj), lambda i, j: (l, 0, nj + j)),
            pl.BlockSpec((None, tj, D_MODEL), lambda i, j: (l, j, 0)),
            _layer_vec(l), _layer_vec(l)],
        out_specs=row,
        out_shape=jax.ShapeDtypeStruct((n_rows, D_MODEL), F32),
        scratch_shapes=[pltpu.VMEM((tm, D_MODEL), BF16)],
        compiler_params=_params(("arbitrary", "arbitrary")),
        name="ffn",
    )(x, ada, ada, ada, ada, ada, ada, w_in, w_in, w_out, ln_g.reshape(DEPTH, 1, -1), ln_b.reshape(DEPTH, 1, -1))


def _rope_tables(both_halves):
    rot = SWA_HEAD_DIM
    half = rot // 2
    n_rows = SEQ // GRID_W
    row = jnp.repeat(jnp.arange(n_rows, dtype=F32), GRID_W)
    colp = jnp.tile(jnp.arange(GRID_W, dtype=F32), n_rows)
    inv = ROPE_BASE ** (-jnp.arange(0, half, 2, dtype=F32) / half)
    ang_r = row[:, None] * inv
    ang_c = colp[:, None] * inv
    cos = jnp.concatenate([jnp.cos(ang_r), jnp.cos(ang_r), jnp.cos(ang_c), jnp.cos(ang_c)], -1)
    sin = jnp.concatenate([-jnp.sin(ang_r), jnp.sin(ang_r), -jnp.sin(ang_c), jnp.sin(ang_c)], -1)
    if both_halves:
        cos = jnp.concatenate([cos, cos], -1)
        sin = jnp.concatenate([sin, sin], -1)
    else:
        cos = jnp.concatenate([cos, jnp.ones_like(cos)], -1)
        sin = jnp.concatenate([sin, jnp.zeros_like(sin)], -1)
    cos = jnp.concatenate([jnp.ones((CTX_LEN, LANES), F32), cos], 0)
    sin = jnp.concatenate([jnp.zeros((CTX_LEN, LANES), F32), sin], 0)
    return cos, sin


def _mla_weights(w_q_up, w_kv_up):
    wq = w_q_up.reshape(DEPTH, MLA_Q_RANK, MLA_HEADS, MLA_NOPE + MLA_ROPE)
    wq = jnp.pad(wq, ((0, 0), (0, 0), (0, 0), (0, MLA_SLOT - MLA_NOPE - MLA_ROPE)))
    wkv = w_kv_up.reshape(DEPTH, MLA_KV_RANK, MLA_HEADS, MLA_NOPE + MLA_V)
    wk = wkv[..., :MLA_NOPE].reshape(DEPTH, MLA_KV_RANK, MLA_HEADS * MLA_NOPE)
    wv = wkv[..., MLA_NOPE:].reshape(DEPTH, MLA_KV_RANK, MLA_HEADS * MLA_V)
    return wq.reshape(DEPTH, MLA_Q_RANK, MLA_HEADS * MLA_SLOT).astype(BF16), wk.astype(BF16), wv.astype(BF16)


def kernel(x, c, ctx, c_ctx, w_ada, b_ada, w_in, conv_w, conv_b, lru_wr, lru_br, lru_wi, lru_bi, lru_lambda, mla_q_norm, mla_w_q_up, mla_kv_norm, mla_w_kv_up, swa_sinks, w_branch, w_out, ln1_g, ln1_b, w_ffn_in, w_ffn_out, ln2_g, ln2_b):
    cos_mla, sin_mla = _rope_tables(False)
    cos_swa, sin_swa = _rope_tables(True)
    cvec = jnp.concatenate([c, c_ctx[None], jnp.zeros((SUBLANES - BATCH - 1, D_MODEL), F32)], 0)
    ada = _ada(cvec, w_ada, b_ada).reshape(DEPTH, 6, SUBLANES, 1, D_MODEL)
    w_p = jnp.transpose(w_in, (0, 2, 1)).astype(BF16)
    wr, wi = lru_wr.astype(BF16), lru_wi.astype(BF16)
    wq_p, wk_p, wv_p = _mla_weights(mla_w_q_up, mla_w_kv_up)
    wb, wo = w_branch.astype(BF16), w_out.astype(BF16)
    wf_in, wf_out = w_ffn_in.astype(BF16), w_ffn_out.astype(BF16)
    xs = jnp.concatenate([ctx, x], 1).reshape(ROWS, D_MODEL)
    for l in range(DEPTH):
        last = l == DEPTH - 1
        p = _mixin(l, xs, ada, w_p)
        ya = _lru(l, p, conv_w, conv_b, wr, lru_br, wi, lru_bi, lru_lambda)
        q, k, v = _mla_prep(l, p, mla_q_norm, wq_p, mla_kv_norm, wk_p, wv_p, cos_mla, sin_mla)
        yb = _mla_attn(q, k, v)
        yc = _swa(l, p, cos_swa, sin_swa, swa_sinks)
        x1 = _merge(l, ya, yb, yc, p, wb, wo, xs, ada, ln1_g, ln1_b, last)
        xs = _ffn(l, x1, ada, wf_in, wf_out, ln2_g, ln2_b, last)
    return xs.reshape(BATCH, SEQ, D_MODEL)
```

```python
import functools
import math

import jax
import jax.numpy as jnp
from jax import lax
from jax.experimental import pallas as pl
from jax.experimental.pallas import tpu as pltpu

F32 = jnp.float32
BF16 = jnp.bfloat16

D_MODEL = 2048
BATCH = 4
SEQ = 2048
DEPTH = 2
GRID_W = 64
CTX_LEN = 256
LRU_WIDTH = 1024
LRU_BLOCKS = 8
LRU_BLOCK_W = LRU_WIDTH // LRU_BLOCKS
LRU_C = 8.0
CONV_W = 4
MLA_HEADS = 8
MLA_Q_RANK = 512
MLA_KV_RANK = 256
MLA_NOPE = 128
MLA_ROPE = 64
MLA_V = 128
MLA_SCALE = (MLA_NOPE + MLA_ROPE) ** -0.5
SWA_HEADS = 16
SWA_KV_HEADS = 4
SWA_HEAD_DIM = 64
SWA_REP = SWA_HEADS // SWA_KV_HEADS
SWA_SCALE = SWA_HEAD_DIM ** -0.5
WINDOW = 128
N_BRANCH = 3
BRANCH_W = 1024
MIX_IN = 4416
FFN_HIDDEN = -(-8 * D_MODEL // (3 * 256)) * 256
ROPE_BASE = 10000.0
LN_EPS = 1e-5
RMS_EPS = 1e-6
ALPHA = (2 * DEPTH) ** 0.25

T_TOK = CTX_LEN + SEQ
ROWS = BATCH * T_TOK
LANES = 128
SUBLANES = 8
MLA_SLOT = 2 * LANES

P_AX, P_AG, P_BQ, P_BKV, P_BKR = 0, 1024, 2048, 2560, 2816
P_CQ, P_CK, P_CV, P_GATE = 3072, 4096, 4352, 4608
P_COLS = P_GATE + N_BRANCH * D_MODEL

VMEM_LIMIT = 56 * 1024 * 1024

TM_MIX = 1152
TN_MIX = 1536
TM_MERGE = 576
TJ_MERGE = 512
TM_FFN = 768
TJ_FFN = 512
TM_PREP = 576
TQ_ATT = 256
TQ_MLA = 576
TN_ADA = 2048
TM_MERGE_LATENT = 512
TM_FFN_LATENT = 1024
LOG2E = math.log2(math.e)


def _params(sem):
    return pltpu.CompilerParams(dimension_semantics=sem, vmem_limit_bytes=VMEM_LIMIT)


def _row_select(i, tiles_per_batch, tm, ctx_vec, batch_vec):
    rows = lax.broadcasted_iota(jnp.int32, (tm, 1), 0)
    n_ctx = jnp.where(i % tiles_per_batch == 0, CTX_LEN, 0)
    return jnp.where(rows < n_ctx, ctx_vec, batch_vec)


def _layer_norm(v, g, b):
    mu = jnp.mean(v, -1, keepdims=True)
    d = v - mu
    var = jnp.mean(d * d, -1, keepdims=True)
    return d * lax.rsqrt(var + LN_EPS) * g + b


def _rms_norm(v, g):
    return v * lax.rsqrt(jnp.mean(v * v, -1, keepdims=True) + RMS_EPS) * g


def _swap_pairs(x):
    lane = lax.broadcasted_iota(jnp.int32, (1, LANES), 1)
    first = (lane & 16) == 0
    return jnp.where(first, pltpu.roll(x, LANES - 16, 1), pltpu.roll(x, 16, 1))


def _rope(x, cos, sin):
    return x * cos + _swap_pairs(x) * sin


def _ada_kernel(c_ref, w_ref, b_ref, o_ref, *, chunks_per_vec):
    j = pl.program_id(1)
    c = c_ref[...]
    a = (c * jax.nn.sigmoid(c)).astype(BF16)
    acc = jnp.dot(a, w_ref[0].astype(BF16), preferred_element_type=F32)
    vec = j // chunks_per_vec
    one = jnp.where((vec == 1) | (vec == 4), 1.0, 0.0)
    o_ref[0, 0] = acc + b_ref[0] + one


def _ada(cvec, w_ada, b_ada):
    tn = TN_ADA
    cpv = D_MODEL // tn
    return pl.pallas_call(
        functools.partial(_ada_kernel, chunks_per_vec=cpv),
        grid=(DEPTH, 6 * cpv),
        in_specs=[
            pl.BlockSpec((SUBLANES, D_MODEL), lambda l, j: (0, 0)),
            pl.BlockSpec((1, D_MODEL, tn), lambda l, j: (l, 0, j)),
            pl.BlockSpec((1, 1, tn), lambda l, j: (l, 0, j)),
        ],
        out_specs=pl.BlockSpec((1, 1, SUBLANES, tn), lambda l, j: (l, j // cpv, 0, j % cpv)),
        out_shape=jax.ShapeDtypeStruct((DEPTH, 6, SUBLANES, D_MODEL), F32),
        compiler_params=_params(("arbitrary", "arbitrary")),
        name="ada",
    )(cvec, w_ada, b_ada.reshape(DEPTH, 1, 6 * D_MODEL))


def _vec_specs(l, vec, tiles_per_batch):
    return [
        pl.BlockSpec((None, 1, 1, 1, D_MODEL), lambda i, j: (l, vec, i // tiles_per_batch, 0, 0)),
        pl.BlockSpec((None, 1, 1, 1, D_MODEL), lambda i, j: (l, vec, BATCH, 0, 0)),
    ]


def _layer_vec(l):
    return pl.BlockSpec((None, 1, D_MODEL), lambda i, j: (l, 0, 0))


def _row_spec(tm, width, col_block, latent_only):
    if not latent_only:
        return pl.BlockSpec((tm, width), lambda i, j: (i, col_block(j)))
    tiles = SEQ // tm

    def index(i, j):
        row = (i // tiles) * T_TOK + CTX_LEN + (i % tiles) * tm
        return pl.multiple_of(row, CTX_LEN), pl.multiple_of(col_block(j) * width, LANES)

    return pl.BlockSpec((pl.Element(tm), pl.Element(width)), index)


def _call_with_casts(body, *, grid, in_specs, out_spec, out_shape, args, casts, name, scratch_shapes=()):
    n_in, n_side = len(in_specs), len(casts)
    n_steps = math.prod(grid)
    side_in, side_out, side_shape = [], [], []
    for src, layer, n_blocks in casts:
        _, n_rows, n_cols = src.shape
        rows = n_rows // n_blocks
        assert rows * n_blocks == n_rows and rows % (2 * SUBLANES) == 0 and n_blocks <= n_steps

        def block(*ids, n_blocks=n_blocks):
            step = ids[0] if len(ids) == 1 else ids[0] * grid[1] + ids[1]
            return jnp.minimum(step, n_blocks - 1)

        side_in.append(pl.BlockSpec((None, rows, n_cols), lambda *ids, block=block, layer=layer: (layer, block(*ids), 0)))
        side_out.append(pl.BlockSpec((rows, n_cols), lambda *ids, block=block: (block(*ids), 0)))
        side_shape.append(jax.ShapeDtypeStruct((n_rows, n_cols), BF16))

    def kernel(*refs):
        for src_ref, dst_ref in zip(refs[n_in:n_in + n_side], refs[n_in + n_side + 1:n_in + 2 * n_side + 1]):
            dst_ref[...] = src_ref[...].astype(dst_ref.dtype)
        body(*refs[:n_in], refs[n_in + n_side], *refs[n_in + 2 * n_side + 1:])

    out = pl.pallas_call(
        kernel,
        grid=grid,
        in_specs=list(in_specs) + side_in,
        out_specs=[out_spec] + side_out,
        out_shape=[out_shape] + side_shape,
        scratch_shapes=list(scratch_shapes),
        compiler_params=_params(("arbitrary",) * len(grid)),
        name=name,
    )(*args, *[c[0] for c in casts])
    return out[0], list(out[1:])


def _mixin_kernel(x_ref, scb_ref, scc_ref, shb_ref, shc_ref, w_ref, o_ref, u_ref, *, tpb, tm):
    i = pl.program_id(0)

    @pl.when(pl.program_id(1) == 0)
    def _():
        sc = _row_select(i, tpb, tm, scc_ref[0, 0], scb_ref[0, 0])
        sh = _row_select(i, tpb, tm, shc_ref[0, 0], shb_ref[0, 0])
        u_ref[...] = (x_ref[...] * sc + sh).astype(BF16)

    o_ref[...] = lax.dot_general(u_ref[...], w_ref[...], (((1,), (1,)), ((), ())),
                                 preferred_element_type=F32).astype(o_ref.dtype)


def _mixin(l, x, ada, w_t, casts):
    tm, tn = TM_MIX, TN_MIX
    tpb = T_TOK // tm
    pad = P_CQ - (P_BKR + MLA_ROPE)
    assert P_CQ % tn == 0 and w_t.shape[0] + pad == P_COLS

    def w_index(i, j):
        row = j * tn - jnp.where(j >= P_CQ // tn, pad, 0)
        return pl.multiple_of(row, MLA_ROPE), 0

    return _call_with_casts(
        functools.partial(_mixin_kernel, tpb=tpb, tm=tm),
        grid=(ROWS // tm, P_COLS // tn),
        in_specs=[pl.BlockSpec((tm, D_MODEL), lambda i, j: (i, 0))]
        + _vec_specs(l, 1, tpb)
        + _vec_specs(l, 0, tpb)
        + [pl.BlockSpec((pl.Element(tn), pl.Element(D_MODEL)), w_index)],
        out_spec=pl.BlockSpec((tm, tn), lambda i, j: (i, j)),
        out_shape=jax.ShapeDtypeStruct((ROWS, P_COLS), BF16),
        scratch_shapes=[pltpu.VMEM((tm, D_MODEL), BF16)],
        args=(x, ada, ada, ada, ada, w_t),
        casts=casts,
        name="mix_in",
    )


LRU_NB = 2
LRU_SEG_FWD = 292
LRU_SEG_BWD = 260
LRU_PAD_TOK = SUBLANES * LRU_SEG_FWD
assert LRU_PAD_TOK >= T_TOK and CTX_LEN + SUBLANES * LRU_SEG_BWD == LRU_PAD_TOK
assert LRU_SEG_FWD % 8 == 4 and LRU_SEG_BWD % 8 == 4


def _lru_kernel(ax_ref, ag_ref, cw_ref, cb_ref, wr_ref, br_ref, wi_ref, bi_ref, lam_ref, y_ref,
                a_s, b_s, h_s, p_s):
    n_tok = T_TOK
    nb = LRU_NB
    t = lax.broadcasted_iota(jnp.int32, (n_tok, 1), 0)
    is_ctx = t < CTX_LEN
    t_loc = jnp.where(is_ctx, t, t - CTX_LEN)
    seg_len = jnp.where(is_ctx, CTX_LEN, SEQ)

    for k in range(nb):
        lanes = slice(LANES * k, LANES * (k + 1))
        x = ax_ref[:, lanes].astype(F32)
        xm1 = jnp.where(t_loc >= 1, pltpu.roll(x, 1, 0), 0.0)
        xm2 = jnp.where(t_loc >= 2, pltpu.roll(x, 2, 0), 0.0)
        xp1 = jnp.where(t_loc <= seg_len - 2, pltpu.roll(x, n_tok - 1, 0), 0.0)
        w = cw_ref[:, lanes]
        xc = xm2 * w[0:1] + xm1 * w[1:2] + x * w[2:3] + xp1 * w[3:4] + cb_ref[:, lanes]
        xb = xc.astype(BF16)
        for d in range(2):
            i = d * nb + k
            tr = jnp.tanh(0.5 * (jnp.dot(xb, wr_ref[d, k], preferred_element_type=F32) + br_ref[d, :, lanes]))
            ti = jnp.tanh(0.5 * (jnp.dot(xb, wi_ref[d, k], preferred_element_type=F32) + bi_ref[d, :, lanes]))
            z = -lam_ref[d, :, lanes]
            softplus = jnp.maximum(z, 0.0) + jnp.log1p(jnp.exp(-jnp.abs(z)))
            half_c = (-0.5 * LRU_C) * softplus
            log_a = half_c * tr + half_c
            a = jnp.exp(log_a)
            v = jnp.tanh(-log_a) * (a * a + 1.0)
            mult = jnp.where(v > 0.0, v * lax.rsqrt(v), 0.0)
            a_s[i, 0:n_tok, :] = a
            b_s[i, 0:n_tok, :] = (0.5 * mult) * (ti + 1.0) * xc
            a_s[i, n_tok:LRU_PAD_TOK, :] = jnp.ones((LRU_PAD_TOK - n_tok, LANES), F32)
            b_s[i, n_tok:LRU_PAD_TOK, :] = jnp.zeros((LRU_PAD_TOK - n_tok, LANES), F32)

    sub = lax.broadcasted_iota(jnp.int32, (CTX_LEN, 1), 0) & (SUBLANES - 1)
    for k in range(nb):
        a = a_s[nb + k, 0:CTX_LEN, :]
        bb = b_s[nb + k, 0:CTX_LEN, :]
        for s in (1, 2, 4):
            m = sub < SUBLANES - s
            a_sh = pltpu.roll(a, CTX_LEN - s, 0)
            b_sh = pltpu.roll(bb, CTX_LEN - s, 0)
            bb = jnp.where(m, bb + a * b_sh, bb)
            a = jnp.where(m, a * a_sh, a)
        a_s[nb + k, 0:CTX_LEN, :] = a
        b_s[nb + k, 0:CTX_LEN, :] = bb

    def ctx_bwd(j, hs):
        rows = pl.ds(pl.multiple_of((CTX_LEN // SUBLANES - 1 - j) * SUBLANES, SUBLANES), SUBLANES)
        out = []
        for k in range(nb):
            hg = b_s[nb + k, rows, :] + a_s[nb + k, rows, :] * hs[k]
            b_s[nb + k, rows, :] = hg
            out.append(hg[0:1])
        return tuple(out)

    h_ctx0 = lax.fori_loop(0, CTX_LEN // SUBLANES, ctx_bwd, (jnp.zeros((1, LANES), F32),) * nb, unroll=4)

    def seg_rows(base, s, seg):
        return pl.ds(base + s, SUBLANES, stride=seg)

    def scan_step(i, rows, h, p):
        av = a_s[i, rows, :]
        h = av * h + b_s[i, rows, :]
        p = av * p
        h_s[i, rows, :] = h
        p_s[i, rows, :] = p
        return h, p

    def fwd_steps(s, c):
        rows = seg_rows(0, s, LRU_SEG_FWD)
        return sum((scan_step(k, rows, c[2 * k], c[2 * k + 1]) for k in range(nb)), ())

    def bwd_steps(s, c):
        rows = seg_rows(CTX_LEN, LRU_SEG_BWD - 1 - s, LRU_SEG_BWD)
        return sum((scan_step(nb + k, rows, c[2 * k], c[2 * k + 1]) for k in range(nb)), ())

    zeros8 = jnp.zeros((SUBLANES, LANES), F32)
    ones8 = jnp.ones((SUBLANES, LANES), F32)
    init = (zeros8, ones8) * nb

    def both(s, c):
        return fwd_steps(s, c[:2 * nb]) + bwd_steps(s, c[2 * nb:])

    c = lax.fori_loop(0, LRU_SEG_BWD, both, init + init, unroll=2)
    cf_state = lax.fori_loop(LRU_SEG_BWD, LRU_SEG_FWD, fwd_steps, c[:2 * nb], unroll=2)
    cb_state = c[2 * nb:]

    seg = lax.broadcasted_iota(jnp.int32, (SUBLANES, 1), 0)
    cfs, cbs = [], []
    for k in range(nb):
        hf, pf = cf_state[2 * k], cf_state[2 * k + 1]
        hb, pb = cb_state[2 * k], cb_state[2 * k + 1]
        cf = zeros8
        cb = jnp.where(seg == SUBLANES - 1, h_ctx0[k], 0.0)
        for _ in range(SUBLANES - 1):
            cf = jnp.where(seg >= 1, pltpu.roll(hf + pf * cf, 1, 0), 0.0)
            cb = jnp.where(seg <= SUBLANES - 2, pltpu.roll(hb + pb * cb, SUBLANES - 1, 0), h_ctx0[k])
        cfs.append(cf)
        cbs.append(cb)

    def fix_fwd(s):
        rows = seg_rows(0, s, LRU_SEG_FWD)
        for k in range(nb):
            b_s[k, rows, :] = h_s[k, rows, :] + p_s[k, rows, :] * cfs[k]

    def fix_bwd(s):
        rows = seg_rows(CTX_LEN, s, LRU_SEG_BWD)
        for k in range(nb):
            b_s[nb + k, rows, :] = h_s[nb + k, rows, :] + p_s[nb + k, rows, :] * cbs[k]

    def fix_both(s, c):
        fix_fwd(s)
        fix_bwd(s)
        return c

    def fix_tail(s, c):
        fix_fwd(s)
        return c

    lax.fori_loop(0, LRU_SEG_BWD, fix_both, 0, unroll=2)
    lax.fori_loop(LRU_SEG_BWD, LRU_SEG_FWD, fix_tail, 0, unroll=2)

    for k in range(nb):
        lanes = slice(LANES * k, LANES * (k + 1))
        g = ag_ref[:, lanes].astype(F32)
        cdf = 0.5 * (1.0 + jnp.tanh(math.sqrt(2.0 / math.pi) * (g + 0.044715 * (g * g * g))))
        h_sum = b_s[k, 0:n_tok, :] + b_s[nb + k, 0:n_tok, :]
        y_ref[:, lanes] = (h_sum * (g * cdf)).astype(y_ref.dtype)


def _lru(l, p, conv_w, conv_b, wr, br, wi, bi, lam):
    assert LRU_BLOCK_W == LANES
    cw = LRU_NB * LANES
    col = lambda off: (lambda b, n: (b, off // cw + n))
    vec2 = pl.BlockSpec((None, 2, 1, cw), lambda b, n: (l, 0, 0, n))
    mat = pl.BlockSpec((None, 2, LRU_NB, LRU_BLOCK_W, LRU_BLOCK_W), lambda b, n: (l, 0, n, 0, 0))
    scan_buf = pltpu.VMEM((2 * LRU_NB, LRU_PAD_TOK, LANES), F32)
    return pl.pallas_call(
        _lru_kernel,
        grid=(BATCH, LRU_WIDTH // cw),
        in_specs=[
            pl.BlockSpec((T_TOK, cw), col(P_AX)),
            pl.BlockSpec((T_TOK, cw), col(P_AG)),
            pl.BlockSpec((None, CONV_W, cw), lambda b, n: (l, 0, n)),
            pl.BlockSpec((None, 1, cw), lambda b, n: (l, 0, n)),
            mat, vec2, mat, vec2, vec2,
        ],
        out_specs=pl.BlockSpec((T_TOK, cw), lambda b, n: (b, n)),
        out_shape=jax.ShapeDtypeStruct((ROWS, LRU_WIDTH), BF16),
        scratch_shapes=[scan_buf, scan_buf, scan_buf, scan_buf],
        compiler_params=_params(("arbitrary", "arbitrary")),
        name="rg_lru",
    )(p, p, conv_w, conv_b.reshape(DEPTH, 1, LRU_WIDTH), wr, br.reshape(DEPTH, 2, 1, LRU_WIDTH), wi,
      bi.reshape(DEPTH, 2, 1, LRU_WIDTH), lam.reshape(DEPTH, 2, 1, LRU_WIDTH))


def _mla_prep_kernel(bq_ref, bkv_ref, bkr_ref, qn_ref, wq_ref, kvn_ref, wk_ref, wv_ref, cos_ref, sin_ref,
                     q_out, k_out, v_out):
    cos = cos_ref[...]
    sin = sin_ref[...]
    qn = _rms_norm(bq_ref[...].astype(F32), qn_ref[...])
    q = jnp.dot(qn.astype(BF16), wq_ref[...], preferred_element_type=F32) * (MLA_SCALE * LOG2E)
    kvn = _rms_norm(bkv_ref[...].astype(F32), kvn_ref[...]).astype(BF16)
    kn = jnp.dot(kvn, wk_ref[...], preferred_element_type=F32).astype(BF16)
    v_out[...] = jnp.dot(kvn, wv_ref[...], preferred_element_type=F32).astype(BF16)
    lane = lax.broadcasted_iota(jnp.int32, (1, LANES), 1)
    kr = jnp.where(lane < MLA_ROPE, bkr_ref[...].astype(F32), 0.0)
    kr = _rope(kr, cos, sin).astype(BF16)
    for h in range(MLA_HEADS):
        lo = MLA_SLOT * h
        q_out[:, lo:lo + LANES] = q[:, lo:lo + LANES].astype(BF16)
        q_out[:, lo + LANES:lo + MLA_SLOT] = _rope(q[:, lo + LANES:lo + MLA_SLOT], cos, sin).astype(BF16)
        k_out[:, lo:lo + LANES] = kn[:, LANES * h:LANES * (h + 1)]
        k_out[:, lo + LANES:lo + MLA_SLOT] = kr


def _mla_prep(l, p, q_norm, wq_p, kv_norm, wk_p, wv_p, cos, sin):
    tm = TM_PREP
    tpb = T_TOK // tm
    full = lambda shape: pl.BlockSpec((None,) + shape, lambda i: (l,) + (0,) * len(shape))
    return pl.pallas_call(
        _mla_prep_kernel,
        grid=(ROWS // tm,),
        in_specs=[
            pl.BlockSpec((tm, MLA_Q_RANK), lambda i: (i, P_BQ // MLA_Q_RANK)),
            pl.BlockSpec((tm, MLA_KV_RANK), lambda i: (i, P_BKV // MLA_KV_RANK)),
            pl.BlockSpec((tm, LANES), lambda i: (i, P_BKR // LANES)),
            full((1, MLA_Q_RANK)),
            full((MLA_Q_RANK, MLA_HEADS * MLA_SLOT)),
            full((1, MLA_KV_RANK)),
            full((MLA_KV_RANK, MLA_HEADS * MLA_NOPE)),
            full((MLA_KV_RANK, MLA_HEADS * MLA_V)),
            pl.BlockSpec((tm, LANES), lambda i: (i % tpb, 0)),
            pl.BlockSpec((tm, LANES), lambda i: (i % tpb, 0)),
        ],
        out_specs=[
            pl.BlockSpec((tm, MLA_HEADS * MLA_SLOT), lambda i: (i, 0)),
            pl.BlockSpec((tm, MLA_HEADS * MLA_SLOT), lambda i: (i, 0)),
            pl.BlockSpec((tm, MLA_HEADS * MLA_V), lambda i: (i, 0)),
        ],
        out_shape=[
            jax.ShapeDtypeStruct((ROWS, MLA_HEADS * MLA_SLOT), BF16),
            jax.ShapeDtypeStruct((ROWS, MLA_HEADS * MLA_SLOT), BF16),
            jax.ShapeDtypeStruct((ROWS, MLA_HEADS * MLA_V), BF16),
        ],
        compiler_params=_params(("arbitrary",)),
        name="mla_prep",
    )(p, p, p, q_norm.reshape(DEPTH, 1, -1), wq_p, kv_norm.reshape(DEPTH, 1, -1), wk_p, wv_p, cos, sin)


def _mla_attn_kernel(q_ref, k_ref, v_ref, o_ref):
    tq = q_ref.shape[0]

    def attend(row0, n_rows, n_keys):
        for h in range(MLA_HEADS):
            q = q_ref[row0:row0 + n_rows, MLA_SLOT * h:MLA_SLOT * (h + 1)]
            k = k_ref[0:n_keys, MLA_SLOT * h:MLA_SLOT * (h + 1)]
            s = lax.dot_general(q, k, (((1,), (1,)), ((), ())), preferred_element_type=F32)
            m = jnp.max(s, -1, keepdims=True)
            p = jnp.exp2(s - m)
            l = jnp.sum(p, -1, keepdims=True)
            o = jnp.dot(p.astype(BF16), v_ref[0:n_keys, MLA_V * h:MLA_V * (h + 1)], preferred_element_type=F32)
            o_ref[row0:row0 + n_rows, MLA_V * h:MLA_V * (h + 1)] = (o / l).astype(o_ref.dtype)

    @pl.when(pl.program_id(1) == 0)
    def _():
        attend(0, CTX_LEN, CTX_LEN)
        attend(CTX_LEN, tq - CTX_LEN, T_TOK)

    @pl.when(pl.program_id(1) > 0)
    def _():
        attend(0, tq, T_TOK)


def _mla_attn(q, k, v):
    tq = TQ_MLA
    assert tq > CTX_LEN and T_TOK % tq == 0 and tq % (2 * SUBLANES) == 0
    tpb = T_TOK // tq
    per_batch = lambda width: pl.BlockSpec((T_TOK, width), lambda b, i: (b, 0), pipeline_mode=pl.Buffered(1))
    return pl.pallas_call(
        _mla_attn_kernel,
        grid=(BATCH, tpb),
        in_specs=[
            pl.BlockSpec((tq, MLA_HEADS * MLA_SLOT), lambda b, i: (b * tpb + i, 0)),
            per_batch(MLA_HEADS * MLA_SLOT),
            per_batch(MLA_HEADS * MLA_V),
        ],
        out_specs=pl.BlockSpec((tq, MLA_HEADS * MLA_V), lambda b, i: (b * tpb + i, 0)),
        out_shape=jax.ShapeDtypeStruct((ROWS, MLA_HEADS * MLA_V), BF16),
        compiler_params=_params(("arbitrary", "arbitrary")),
        name="mla_attn",
    )(q, k, v)


def _dup_half(x, upper):
    lane = lax.broadcasted_iota(jnp.int32, (1, LANES), 1)
    r = pltpu.roll(x, LANES // 2, 1)
    out = jnp.where(lane < LANES // 2, r, x) if upper else jnp.where(lane < LANES // 2, x, r)
    return out.astype(BF16)


def _swa_kernel(q_ref, k_ref, v_ref, cos_ref, sin_ref, sink_ref, o_ref, *, layer):
    tq = TQ_ATT
    band = tq + 2 * WINDOW
    qt = pl.program_id(1)
    q_row = pl.multiple_of(qt * tq, tq)
    lane = lax.broadcasted_iota(jnp.int32, (1, LANES), 1)
    low = lane < LANES // 2
    cos_q = cos_ref[pl.ds(q_row, tq), :]
    sin_q = sin_ref[pl.ds(q_row, tq), :]

    def q_block(c):
        x = q_ref[:, LANES * c:LANES * (c + 1)].astype(F32)
        return _rope(x, cos_q, sin_q) * (SWA_SCALE * LOG2E)

    def softmax_out(qm, sink, keys, vals, masks):
        ss = []
        for kk, mk in zip(keys, masks):
            s = lax.dot_general(qm, kk, (((1,), (1,)), ((), ())), preferred_element_type=F32)
            ss.append(s if mk is None else jnp.where(mk, s, -jnp.inf))
        m = jnp.maximum(functools.reduce(jnp.maximum, [jnp.max(s, -1, keepdims=True) for s in ss]), sink)
        ps = [jnp.exp2(s - m) for s in ss]
        l = functools.reduce(jnp.add, [jnp.sum(p, -1, keepdims=True) for p in ps]) + jnp.exp2(sink - m)
        o = functools.reduce(jnp.add, [jnp.dot(p.astype(BF16), vv, preferred_element_type=F32)
                                       for p, vv in zip(ps, vals)])
        return o / l

    def run(keys_of, vals_of, masks):
        for g in range(SWA_KV_HEADS):
            keys = keys_of(g)
            vals = vals_of(g)
            qms, sinks = [], []
            for c in (2 * g, 2 * g + 1):
                qc = q_block(c)
                qms += [jnp.where(low, qc, 0.0).astype(BF16), jnp.where(low, 0.0, qc).astype(BF16)]
                sinks += [sink_ref[layer, 2 * c] * LOG2E, sink_ref[layer, 2 * c + 1] * LOG2E]
            scores = [[lax.dot_general(qm, kk, (((1,), (1,)), ((), ())), preferred_element_type=F32) for qm in qms]
                      for kk in keys]
            ps, ls = [], []
            for h in range(SWA_REP):
                ss = [s[h] if mk is None else jnp.where(mk, s[h], -jnp.inf) for s, mk in zip(scores, masks)]
                m = jnp.maximum(functools.reduce(jnp.maximum, [jnp.max(s, -1, keepdims=True) for s in ss]), sinks[h])
                pr = [jnp.exp2(s - m) for s in ss]
                ls.append(functools.reduce(jnp.add, [jnp.sum(p, -1, keepdims=True) for p in pr]) + jnp.exp2(sinks[h] - m))
                ps.append([p.astype(BF16) for p in pr])
            outs = [[jnp.dot(ps[h][b], vv, preferred_element_type=F32) for h in range(SWA_REP)]
                    for b, vv in enumerate(vals)]
            os_ = [functools.reduce(jnp.add, [outs[b][h] for b in range(len(vals))]) / ls[h] for h in range(SWA_REP)]
            for i, c in enumerate((2 * g, 2 * g + 1)):
                o_ref[:, LANES * c:LANES * (c + 1)] = jnp.where(low, os_[2 * i], os_[2 * i + 1]).astype(o_ref.dtype)

    def ctx_kv(ref, g):
        return _dup_half(ref[0:CTX_LEN, LANES * (g // 2):LANES * (g // 2 + 1)].astype(F32), g % 2 == 1)

    @pl.when(qt == 0)
    def _():
        run(lambda g: [ctx_kv(k_ref, g)], lambda g: [ctx_kv(v_ref, g)], [None])

    @pl.when(qt > 0)
    def _():
        start = pl.multiple_of(jnp.minimum(q_row - WINDOW, T_TOK - band), LANES)
        qpos = q_row - CTX_LEN + lax.broadcasted_iota(jnp.int32, (tq, 1), 0)
        kpos = start - CTX_LEN + lax.broadcasted_iota(jnp.int32, (1, band), 1)
        valid = (jnp.abs(kpos - qpos) <= WINDOW) & (kpos >= 0)
        cos_k = cos_ref[pl.ds(start, band), :]
        sin_k = sin_ref[pl.ds(start, band), :]

        def band_k(g):
            kb = k_ref[pl.ds(start, band), LANES * (g // 2):LANES * (g // 2 + 1)].astype(F32)
            return _dup_half(_rope(kb, cos_k, sin_k), g % 2 == 1)

        def band_v(g):
            return _dup_half(v_ref[pl.ds(start, band), LANES * (g // 2):LANES * (g // 2 + 1)].astype(F32), g % 2 == 1)

        run(lambda g: [ctx_kv(k_ref, g), band_k(g)], lambda g: [ctx_kv(v_ref, g), band_v(g)], [None, valid])


def _swa(l, p, cos, sin, sinks, casts):
    tq = TQ_ATT
    tpb = T_TOK // tq
    kvw = SWA_KV_HEADS * SWA_HEAD_DIM
    return _call_with_casts(
        functools.partial(_swa_kernel, layer=l),
        grid=(BATCH, tpb),
        in_specs=[
            pl.BlockSpec((tq, SWA_HEADS * SWA_HEAD_DIM), lambda b, i: (b * tpb + i, P_CQ // (SWA_HEADS * SWA_HEAD_DIM))),
            pl.BlockSpec((T_TOK, kvw), lambda b, i: (b, P_CK // kvw)),
            pl.BlockSpec((T_TOK, kvw), lambda b, i: (b, P_CV // kvw)),
            pl.BlockSpec((T_TOK, LANES), lambda b, i: (0, 0)),
            pl.BlockSpec((T_TOK, LANES), lambda b, i: (0, 0)),
            pl.BlockSpec(memory_space=pltpu.SMEM),
        ],
        out_spec=pl.BlockSpec((tq, SWA_HEADS * SWA_HEAD_DIM), lambda b, i: (b * tpb + i, 0)),
        out_shape=jax.ShapeDtypeStruct((ROWS, SWA_HEADS * SWA_HEAD_DIM), BF16),
        args=(p, p, p, cos, sin, sinks),
        casts=casts,
        name="swa",
    )


def _gate_rows(i, tpb, tm, gvc_ref, gvb_ref, has_ctx):
    return _row_select(i, tpb, tm, gvc_ref[0, 0], gvb_ref[0, 0]) if has_ctx else gvb_ref[0, 0]


def _merge_kernel(ya_ref, yb_ref, yc_ref, ga_ref, gb_ref, gc_ref, wb_ref, wo_ref, x_ref, gvb_ref, gvc_ref,
                  lng_ref, lnb_ref, o_ref, *, tpb, tm, nj, has_ctx):
    i = pl.program_id(0)
    j = pl.program_id(1)

    @pl.when(j == 0)
    def _():
        o_ref[...] = jnp.zeros_like(o_ref)

    z = None
    for n, (y_ref, g_ref) in enumerate(((ya_ref, ga_ref), (yb_ref, gb_ref), (yc_ref, gc_ref))):
        zn = jnp.dot(y_ref[...], wb_ref[n], preferred_element_type=F32)
        zn = jax.nn.sigmoid(g_ref[...].astype(F32)) * zn
        z = zn if z is None else z + zn
    o_ref[...] += jnp.dot(z.astype(BF16), wo_ref[...], preferred_element_type=F32)

    @pl.when(j == nj - 1)
    def _():
        gate = _gate_rows(i, tpb, tm, gvc_ref, gvb_ref, has_ctx)
        o_ref[...] = _layer_norm(ALPHA * x_ref[...] + gate * o_ref[...], lng_ref[...], lnb_ref[...])


def _merge(l, ya, yb, yc, p, wb, wo, x, ada, ln_g, ln_b, latent_only, casts):
    tm = TM_MERGE_LATENT if latent_only else TM_MERGE
    tj = TJ_MERGE
    tpb = (SEQ if latent_only else T_TOK) // tm
    n_rows = BATCH * SEQ if latent_only else ROWS
    nj = D_MODEL // tj
    ybs = _row_spec(tm, BRANCH_W, lambda j: 0, latent_only)
    gate = lambda n: _row_spec(tm, tj, lambda j: (P_GATE + n * D_MODEL) // tj + j, latent_only)
    return _call_with_casts(
        functools.partial(_merge_kernel, tpb=tpb, tm=tm, nj=nj, has_ctx=not latent_only),
        grid=(n_rows // tm, nj),
        in_specs=[ybs, ybs, ybs, gate(0), gate(1), gate(2),
                  pl.BlockSpec((N_BRANCH, BRANCH_W, tj), lambda i, j: (0, 0, j)),
                  pl.BlockSpec((tj, D_MODEL), lambda i, j: (j, 0)),
                  _row_spec(tm, D_MODEL, lambda j: 0, latent_only)]
        + _vec_specs(l, 2, tpb) + [_layer_vec(l), _layer_vec(l)],
        out_spec=pl.BlockSpec((tm, D_MODEL), lambda i, j: (i, 0)),
        out_shape=jax.ShapeDtypeStruct((n_rows, D_MODEL), F32),
        args=(ya, yb, yc, p, p, p, wb, wo, x, ada, ada, ln_g.reshape(DEPTH, 1, -1), ln_b.reshape(DEPTH, 1, -1)),
        casts=casts,
        name="merge",
    )


def _ffn_kernel(x_ref, scb_ref, scc_ref, shb_ref, shc_ref, gvb_ref, gvc_ref, wg_ref, wu_ref, wo_ref,
                lng_ref, lnb_ref, o_ref, u_ref, *, tpb, tm, nj, has_ctx):
    i = pl.program_id(0)
    j = pl.program_id(1)

    @pl.when(j == 0)
    def _():
        sc = _gate_rows(i, tpb, tm, scc_ref, scb_ref, has_ctx)
        sh = _gate_rows(i, tpb, tm, shc_ref, shb_ref, has_ctx)
        u_ref[...] = (x_ref[...] * sc + sh).astype(BF16)
        o_ref[...] = jnp.zeros_like(o_ref)

    u = u_ref[...]
    gt = jnp.dot(u, wg_ref[...], preferred_element_type=F32)
    up = jnp.dot(u, wu_ref[...], preferred_element_type=F32)
    h = (gt * jax.nn.sigmoid(gt) * up).astype(BF16)
    o_ref[...] += jnp.dot(h, wo_ref[...], preferred_element_type=F32)

    @pl.when(j == nj - 1)
    def _():
        gate = _gate_rows(i, tpb, tm, gvc_ref, gvb_ref, has_ctx)
        o_ref[...] = _layer_norm(ALPHA * x_ref[...] + gate * o_ref[...], lng_ref[...], lnb_ref[...])


def _ffn(l, x, ada, w_in, w_out, ln_g, ln_b, latent_only, casts):
    tm = TM_FFN_LATENT if latent_only else TM_FFN
    tj = TJ_FFN
    tpb = (SEQ if latent_only else T_TOK) // tm
    n_rows = x.shape[0]
    nj = FFN_HIDDEN // tj
    row = pl.BlockSpec((tm, D_MODEL), lambda i, j: (i, 0))
    row_in = pl.BlockSpec((tm, D_MODEL), lambda i, j: (i, 0), pipeline_mode=pl.Buffered(1))
    return _call_with_casts(
        functools.partial(_ffn_kernel, tpb=tpb, tm=tm, nj=nj, has_ctx=not latent_only),
        grid=(n_rows // tm, nj),
        in_specs=[row_in] + _vec_specs(l, 4, tpb) + _vec_specs(l, 3, tpb) + _vec_specs(l, 5, tpb) + [
            pl.BlockSpec((D_MODEL, tj), lambda i, j: (0, j)),
            pl.BlockSpec((D_MODEL, tj), lambda i, j: (0, nj + j)),
            pl.BlockSpec((tj, D_MODEL), lambda i, j: (j, 0)),
            _layer_vec(l), _layer_vec(l)],
        out_spec=row,
        out_shape=jax.ShapeDtypeStruct((n_rows, D_MODEL), F32),
        scratch_shapes=[pltpu.VMEM((tm, D_MODEL), BF16)],
        args=(x, ada, ada, ada, ada, ada, ada, w_in, w_in, w_out, ln_g.reshape(DEPTH, 1, -1), ln_b.reshape(DEPTH, 1, -1)),
        casts=casts,
        name="ffn",
    )


def _rope_tables(both_halves):
    rot = SWA_HEAD_DIM
    half = rot // 2
    n_rows = SEQ // GRID_W
    row = jnp.repeat(jnp.arange(n_rows, dtype=F32), GRID_W)
    colp = jnp.tile(jnp.arange(GRID_W, dtype=F32), n_rows)
    inv = ROPE_BASE ** (-jnp.arange(0, half, 2, dtype=F32) / half)
    ang_r = row[:, None] * inv
    ang_c = colp[:, None] * inv
    cos = jnp.concatenate([jnp.cos(ang_r), jnp.cos(ang_r), jnp.cos(ang_c), jnp.cos(ang_c)], -1)
    sin = jnp.concatenate([-jnp.sin(ang_r), jnp.sin(ang_r), -jnp.sin(ang_c), jnp.sin(ang_c)], -1)
    if both_halves:
        cos = jnp.concatenate([cos, cos], -1)
        sin = jnp.concatenate([sin, sin], -1)
    else:
        cos = jnp.concatenate([cos, jnp.ones_like(cos)], -1)
        sin = jnp.concatenate([sin, jnp.zeros_like(sin)], -1)
    cos = jnp.concatenate([jnp.ones((CTX_LEN, LANES), F32), cos], 0)
    sin = jnp.concatenate([jnp.zeros((CTX_LEN, LANES), F32), sin], 0)
    return cos, sin


def _mla_weights(w_q_up, w_kv_up):
    wq = w_q_up.reshape(DEPTH, MLA_Q_RANK, MLA_HEADS, MLA_NOPE + MLA_ROPE)
    wq = jnp.pad(wq, ((0, 0), (0, 0), (0, 0), (0, MLA_SLOT - MLA_NOPE - MLA_ROPE)))
    wkv = w_kv_up.reshape(DEPTH, MLA_KV_RANK, MLA_HEADS, MLA_NOPE + MLA_V)
    wk = wkv[..., :MLA_NOPE].reshape(DEPTH, MLA_KV_RANK, MLA_HEADS * MLA_NOPE)
    wv = wkv[..., MLA_NOPE:].reshape(DEPTH, MLA_KV_RANK, MLA_HEADS * MLA_V)
    return wq.reshape(DEPTH, MLA_Q_RANK, MLA_HEADS * MLA_SLOT).astype(BF16), wk.astype(BF16), wv.astype(BF16)


def kernel(x, c, ctx, c_ctx, w_ada, b_ada, w_in, conv_w, conv_b, lru_wr, lru_br, lru_wi, lru_bi, lru_lambda, mla_q_norm, mla_w_q_up, mla_kv_norm, mla_w_kv_up, swa_sinks, w_branch, w_out, ln1_g, ln1_b, w_ffn_in, w_ffn_out, ln2_g, ln2_b):
    cos_mla, sin_mla = _rope_tables(False)
    cos_swa, sin_swa = _rope_tables(True)
    cvec = jnp.concatenate([c, c_ctx[None], jnp.zeros((SUBLANES - BATCH - 1, D_MODEL), F32)], 0)
    ada = _ada(cvec, w_ada, b_ada).reshape(DEPTH, 6, SUBLANES, 1, D_MODEL)
    w_in_t = jnp.transpose(w_in, (0, 2, 1))
    w_t = w_in_t[0].astype(BF16)
    w_branch_rows = w_branch.reshape(DEPTH, N_BRANCH * BRANCH_W, D_MODEL)
    wr, wi = lru_wr.astype(BF16), lru_wi.astype(BF16)
    wq_p, wk_p, wv_p = _mla_weights(mla_w_q_up, mla_w_kv_up)
    xs = jnp.concatenate([ctx, x], 1).reshape(ROWS, D_MODEL)
    for l in range(DEPTH):
        last = l == DEPTH - 1
        p, (wf_out,) = _mixin(l, xs, ada, w_t, [(w_ffn_out, l, 44)])
        ya = _lru(l, p, conv_w, conv_b, wr, lru_br, wi, lru_bi, lru_lambda)
        q, k, v = _mla_prep(l, p, mla_q_norm, wq_p, mla_kv_norm, wk_p, wv_p, cos_mla, sin_mla)
        yb = _mla_attn(q, k, v)
        yc, (wb, wo) = _swa(l, p, cos_swa, sin_swa, swa_sinks, [(w_branch_rows, l, 32), (w_out, l, 32)])
        x1, (wf_in,) = _merge(l, ya, yb, yc, p, wb.reshape(N_BRANCH, BRANCH_W, D_MODEL), wo, xs, ada, ln1_g, ln1_b,
                              last, [(w_ffn_in, l, 64)])
        xs, next_w = _ffn(l, x1, ada, wf_in, wf_out, ln2_g, ln2_b, last, [] if last else [(w_in_t, l + 1, 132)])
        if not last:
            w_t = next_w[0]
    return xs.reshape(BATCH, SEQ, D_MODEL)
```

```python
import functools
import math

import jax
import jax.numpy as jnp
from jax import lax
from jax.experimental import pallas as pl
from jax.experimental.pallas import tpu as pltpu

F32 = jnp.float32
BF16 = jnp.bfloat16

D_MODEL = 2048
BATCH = 4
SEQ = 2048
DEPTH = 2
GRID_W = 64
CTX_LEN = 256
LRU_WIDTH = 1024
LRU_BLOCKS = 8
LRU_BLOCK_W = LRU_WIDTH // LRU_BLOCKS
LRU_C = 8.0
CONV_W = 4
MLA_HEADS = 8
MLA_Q_RANK = 512
MLA_KV_RANK = 256
MLA_NOPE = 128
MLA_ROPE = 64
MLA_V = 128
MLA_SCALE = (MLA_NOPE + MLA_ROPE) ** -0.5
SWA_HEADS = 16
SWA_KV_HEADS = 4
SWA_HEAD_DIM = 64
SWA_REP = SWA_HEADS // SWA_KV_HEADS
SWA_SCALE = SWA_HEAD_DIM ** -0.5
WINDOW = 128
N_BRANCH = 3
BRANCH_W = 1024
MIX_IN = 4416
FFN_HIDDEN = -(-8 * D_MODEL // (3 * 256)) * 256
ROPE_BASE = 10000.0
LN_EPS = 1e-5
RMS_EPS = 1e-6
ALPHA = (2 * DEPTH) ** 0.25

T_TOK = CTX_LEN + SEQ
ROWS = BATCH * T_TOK
LANES = 128
SUBLANES = 8
MLA_SLOT = 2 * LANES

P_AX, P_AG, P_BQ, P_BKV, P_BKR = 0, 1024, 2048, 2560, 2816
P_CQ, P_CK, P_CV, P_GATE = 3072, 4096, 4352, 4608
P_COLS = P_GATE + N_BRANCH * D_MODEL

VMEM_LIMIT = 56 * 1024 * 1024

TM_MIX = 1152
TN_MIX = 1536
TM_MERGE = 576
TJ_MERGE = 512
TM_FFN = 768
TJ_FFN = 512
TM_PREP = 576
TQ_ATT = 256
TQ_MLA = 576
TN_ADA = 2048
TM_MERGE_LATENT = 512
TM_FFN_LATENT = 1024
LOG2E = math.log2(math.e)


def _params(sem):
    return pltpu.CompilerParams(dimension_semantics=sem, vmem_limit_bytes=VMEM_LIMIT)


def _row_select(i, tiles_per_batch, tm, ctx_vec, batch_vec):
    rows = lax.broadcasted_iota(jnp.int32, (tm, 1), 0)
    n_ctx = jnp.where(i % tiles_per_batch == 0, CTX_LEN, 0)
    return jnp.where(rows < n_ctx, ctx_vec, batch_vec)


def _layer_norm(v, g, b):
    mu = jnp.mean(v, -1, keepdims=True)
    d = v - mu
    var = jnp.mean(d * d, -1, keepdims=True)
    return d * lax.rsqrt(var + LN_EPS) * g + b


def _rms_norm(v, g):
    return v * lax.rsqrt(jnp.mean(v * v, -1, keepdims=True) + RMS_EPS) * g


def _swap_pairs(x):
    lane = lax.broadcasted_iota(jnp.int32, (1, LANES), 1)
    first = (lane & 16) == 0
    return jnp.where(first, pltpu.roll(x, LANES - 16, 1), pltpu.roll(x, 16, 1))


def _rope(x, cos, sin):
    return x * cos + _swap_pairs(x) * sin


def _ada_kernel(c_ref, w_ref, b_ref, o_ref, *, chunks_per_vec):
    j = pl.program_id(1)
    c = c_ref[...]
    a = (c * jax.nn.sigmoid(c)).astype(BF16)
    acc = jnp.dot(a, w_ref[0].astype(BF16), preferred_element_type=F32)
    vec = j // chunks_per_vec
    one = jnp.where((vec == 1) | (vec == 4), 1.0, 0.0)
    o_ref[0, 0] = acc + b_ref[0] + one


def _ada(cvec, w_ada, b_ada):
    tn = TN_ADA
    cpv = D_MODEL // tn
    return pl.pallas_call(
        functools.partial(_ada_kernel, chunks_per_vec=cpv),
        grid=(DEPTH, 6 * cpv),
        in_specs=[
            pl.BlockSpec((SUBLANES, D_MODEL), lambda l, j: (0, 0)),
            pl.BlockSpec((1, D_MODEL, tn), lambda l, j: (l, 0, j)),
            pl.BlockSpec((1, 1, tn), lambda l, j: (l, 0, j)),
        ],
        out_specs=pl.BlockSpec((1, 1, SUBLANES, tn), lambda l, j: (l, j // cpv, 0, j % cpv)),
        out_shape=jax.ShapeDtypeStruct((DEPTH, 6, SUBLANES, D_MODEL), F32),
        compiler_params=_params(("arbitrary", "arbitrary")),
        name="ada",
    )(cvec, w_ada, b_ada.reshape(DEPTH, 1, 6 * D_MODEL))


def _vec_specs(l, vec, tiles_per_batch):
    return [
        pl.BlockSpec((None, 1, 1, 1, D_MODEL), lambda i, j: (l, vec, i // tiles_per_batch, 0, 0)),
        pl.BlockSpec((None, 1, 1, 1, D_MODEL), lambda i, j: (l, vec, BATCH, 0, 0)),
    ]


def _layer_vec(l):
    return pl.BlockSpec((None, 1, D_MODEL), lambda i, j: (l, 0, 0))


def _row_spec(tm, width, col_block, latent_only):
    if not latent_only:
        return pl.BlockSpec((tm, width), lambda i, j: (i, col_block(j)))
    tiles = SEQ // tm

    def index(i, j):
        row = (i // tiles) * T_TOK + CTX_LEN + (i % tiles) * tm
        return pl.multiple_of(row, CTX_LEN), pl.multiple_of(col_block(j) * width, LANES)

    return pl.BlockSpec((pl.Element(tm), pl.Element(width)), index)


def _call_with_casts(body, *, grid, in_specs, out_spec, out_shape, args, casts, name, scratch_shapes=()):
    n_in, n_side = len(in_specs), len(casts)
    n_steps = math.prod(grid)
    side_in, side_out, side_shape = [], [], []
    for src, layer, n_blocks in casts:
        _, n_rows, n_cols = src.shape
        rows = n_rows // n_blocks
        assert rows * n_blocks == n_rows and rows % (2 * SUBLANES) == 0 and n_blocks <= n_steps

        def block(*ids, n_blocks=n_blocks):
            step = ids[0] if len(ids) == 1 else ids[0] * grid[1] + ids[1]
            return jnp.minimum(step, n_blocks - 1)

        side_in.append(pl.BlockSpec((None, rows, n_cols), lambda *ids, block=block, layer=layer: (layer, block(*ids), 0)))
        side_out.append(pl.BlockSpec((rows, n_cols), lambda *ids, block=block: (block(*ids), 0)))
        side_shape.append(jax.ShapeDtypeStruct((n_rows, n_cols), BF16))

    def kernel(*refs):
        for src_ref, dst_ref in zip(refs[n_in:n_in + n_side], refs[n_in + n_side + 1:n_in + 2 * n_side + 1]):
            dst_ref[...] = src_ref[...].astype(dst_ref.dtype)
        body(*refs[:n_in], refs[n_in + n_side], *refs[n_in + 2 * n_side + 1:])

    out = pl.pallas_call(
        kernel,
        grid=grid,
        in_specs=list(in_specs) + side_in,
        out_specs=[out_spec] + side_out,
        out_shape=[out_shape] + side_shape,
        scratch_shapes=list(scratch_shapes),
        compiler_params=_params(("arbitrary",) * len(grid)),
        name=name,
    )(*args, *[c[0] for c in casts])
    return out[0], list(out[1:])


def _mixin_kernel(x_ref, scb_ref, scc_ref, shb_ref, shc_ref, w_ref, o_ref, u_ref, *, tpb, tm):
    i = pl.program_id(0)

    @pl.when(pl.program_id(1) == 0)
    def _():
        sc = _row_select(i, tpb, tm, scc_ref[0, 0], scb_ref[0, 0])
        sh = _row_select(i, tpb, tm, shc_ref[0, 0], shb_ref[0, 0])
        u_ref[...] = (x_ref[...] * sc + sh).astype(BF16)

    o_ref[...] = lax.dot_general(u_ref[...], w_ref[...], (((1,), (1,)), ((), ())),
                                 preferred_element_type=F32).astype(o_ref.dtype)


def _mixin(l, x, ada, w_t, casts):
    tm, tn = TM_MIX, TN_MIX
    tpb = T_TOK // tm
    pad = P_CQ - (P_BKR + MLA_ROPE)
    assert P_CQ % tn == 0 and w_t.shape[0] + pad == P_COLS

    def w_index(i, j):
        row = j * tn - jnp.where(j >= P_CQ // tn, pad, 0)
        return pl.multiple_of(row, MLA_ROPE), 0

    return _call_with_casts(
        functools.partial(_mixin_kernel, tpb=tpb, tm=tm),
        grid=(ROWS // tm, P_COLS // tn),
        in_specs=[pl.BlockSpec((tm, D_MODEL), lambda i, j: (i, 0))]
        + _vec_specs(l, 1, tpb)
        + _vec_specs(l, 0, tpb)
        + [pl.BlockSpec((pl.Element(tn), pl.Element(D_MODEL)), w_index)],
        out_spec=pl.BlockSpec((tm, tn), lambda i, j: (i, j)),
        out_shape=jax.ShapeDtypeStruct((ROWS, P_COLS), BF16),
        scratch_shapes=[pltpu.VMEM((tm, D_MODEL), BF16)],
        args=(x, ada, ada, ada, ada, w_t),
        casts=casts,
        name="mix_in",
    )


LRU_NB = 2
LRU_SEG_FWD = 292
LRU_SEG_BWD = 260
LRU_PAD_TOK = SUBLANES * LRU_SEG_FWD
assert LRU_PAD_TOK >= T_TOK and CTX_LEN + SUBLANES * LRU_SEG_BWD == LRU_PAD_TOK
assert LRU_SEG_FWD % 8 == 4 and LRU_SEG_BWD % 8 == 4


def _lru_kernel(ax_ref, ag_ref, cw_ref, cb_ref, wr_ref, br_ref, wi_ref, bi_ref, lam_ref, y_ref,
                a_s, b_s, h_s, p_s):
    n_tok = T_TOK
    nb = LRU_NB
    t = lax.broadcasted_iota(jnp.int32, (n_tok, 1), 0)
    is_ctx = t < CTX_LEN
    t_loc = jnp.where(is_ctx, t, t - CTX_LEN)
    seg_len = jnp.where(is_ctx, CTX_LEN, SEQ)

    for k in range(nb):
        lanes = slice(LANES * k, LANES * (k + 1))
        x = ax_ref[:, lanes].astype(F32)
        xm1 = jnp.where(t_loc >= 1, pltpu.roll(x, 1, 0), 0.0)
        xm2 = jnp.where(t_loc >= 2, pltpu.roll(x, 2, 0), 0.0)
        xp1 = jnp.where(t_loc <= seg_len - 2, pltpu.roll(x, n_tok - 1, 0), 0.0)
        w = cw_ref[:, lanes]
        xc = xm2 * w[0:1] + xm1 * w[1:2] + x * w[2:3] + xp1 * w[3:4] + cb_ref[:, lanes]
        xb = xc.astype(BF16)
        for d in range(2):
            i = d * nb + k
            tr = jnp.tanh(0.5 * (jnp.dot(xb, wr_ref[d, k], preferred_element_type=F32) + br_ref[d, :, lanes]))
            ti = jnp.tanh(0.5 * (jnp.dot(xb, wi_ref[d, k], preferred_element_type=F32) + bi_ref[d, :, lanes]))
            z = -lam_ref[d, :, lanes]
            softplus = jnp.maximum(z, 0.0) + jnp.log1p(jnp.exp(-jnp.abs(z)))
            half_c = (-0.5 * LRU_C) * softplus
            log_a = half_c * tr + half_c
            a = jnp.exp(log_a)
            v = jnp.tanh(-log_a) * (a * a + 1.0)
            mult = jnp.where(v > 0.0, v * lax.rsqrt(v), 0.0)
            a_s[i, 0:n_tok, :] = a
            b_s[i, 0:n_tok, :] = (0.5 * mult) * (ti + 1.0) * xc
            a_s[i, n_tok:LRU_PAD_TOK, :] = jnp.ones((LRU_PAD_TOK - n_tok, LANES), F32)
            b_s[i, n_tok:LRU_PAD_TOK, :] = jnp.zeros((LRU_PAD_TOK - n_tok, LANES), F32)

    sub = lax.broadcasted_iota(jnp.int32, (CTX_LEN, 1), 0) & (SUBLANES - 1)
    for k in range(nb):
        a = a_s[nb + k, 0:CTX_LEN, :]
        bb = b_s[nb + k, 0:CTX_LEN, :]
        for s in (1, 2, 4):
            m = sub < SUBLANES - s
            a_sh = pltpu.roll(a, CTX_LEN - s, 0)
            b_sh = pltpu.roll(bb, CTX_LEN - s, 0)
            bb = jnp.where(m, bb + a * b_sh, bb)
            a = jnp.where(m, a * a_sh, a)
        a_s[nb + k, 0:CTX_LEN, :] = a
        b_s[nb + k, 0:CTX_LEN, :] = bb

    def ctx_bwd(j, hs):
        rows = pl.ds(pl.multiple_of((CTX_LEN // SUBLANES - 1 - j) * SUBLANES, SUBLANES), SUBLANES)
        out = []
        for k in range(nb):
            hg = b_s[nb + k, rows, :] + a_s[nb + k, rows, :] * hs[k]
            b_s[nb + k, rows, :] = hg
            out.append(hg[0:1])
        return tuple(out)

    h_ctx0 = lax.fori_loop(0, CTX_LEN // SUBLANES, ctx_bwd, (jnp.zeros((1, LANES), F32),) * nb, unroll=4)

    def seg_rows(base, s, seg):
        return pl.ds(base + s, SUBLANES, stride=seg)

    def scan_step(i, rows, h, p):
        av = a_s[i, rows, :]
        h = av * h + b_s[i, rows, :]
        p = av * p
        h_s[i, rows, :] = h
        p_s[i, rows, :] = p
        return h, p

    def fwd_steps(s, c):
        rows = seg_rows(0, s, LRU_SEG_FWD)
        return sum((scan_step(k, rows, c[2 * k], c[2 * k + 1]) for k in range(nb)), ())

    def bwd_steps(s, c):
        rows = seg_rows(CTX_LEN, LRU_SEG_BWD - 1 - s, LRU_SEG_BWD)
        return sum((scan_step(nb + k, rows, c[2 * k], c[2 * k + 1]) for k in range(nb)), ())

    zeros8 = jnp.zeros((SUBLANES, LANES), F32)
    ones8 = jnp.ones((SUBLANES, LANES), F32)
    init = (zeros8, ones8) * nb

    def both(s, c):
        return fwd_steps(s, c[:2 * nb]) + bwd_steps(s, c[2 * nb:])

    c = lax.fori_loop(0, LRU_SEG_BWD, both, init + init, unroll=2)
    cf_state = lax.fori_loop(LRU_SEG_BWD, LRU_SEG_FWD, fwd_steps, c[:2 * nb], unroll=2)
    cb_state = c[2 * nb:]

    seg = lax.broadcasted_iota(jnp.int32, (SUBLANES, 1), 0)
    cfs, cbs = [], []
    for k in range(nb):
        hf, pf = cf_state[2 * k], cf_state[2 * k + 1]
        hb, pb = cb_state[2 * k], cb_state[2 * k + 1]
        cf = zeros8
        cb = jnp.where(seg == SUBLANES - 1, h_ctx0[k], 0.0)
        for _ in range(SUBLANES - 1):
            cf = jnp.where(seg >= 1, pltpu.roll(hf + pf * cf, 1, 0), 0.0)
            cb = jnp.where(seg <= SUBLANES - 2, pltpu.roll(hb + pb * cb, SUBLANES - 1, 0), h_ctx0[k])
        cfs.append(cf)
        cbs.append(cb)

    def fix_fwd(s):
        rows = seg_rows(0, s, LRU_SEG_FWD)
        for k in range(nb):
            b_s[k, rows, :] = h_s[k, rows, :] + p_s[k, rows, :] * cfs[k]

    def fix_bwd(s):
        rows = seg_rows(CTX_LEN, s, LRU_SEG_BWD)
        for k in range(nb):
            b_s[nb + k, rows, :] = h_s[nb + k, rows, :] + p_s[nb + k, rows, :] * cbs[k]

    def fix_both(s, c):
        fix_fwd(s)
        fix_bwd(s)
        return c

    def fix_tail(s, c):
        fix_fwd(s)
        return c

    lax.fori_loop(0, LRU_SEG_BWD, fix_both, 0, unroll=2)
    lax.fori_loop(LRU_SEG_BWD, LRU_SEG_FWD, fix_tail, 0, unroll=2)

    for k in range(nb):
        lanes = slice(LANES * k, LANES * (k + 1))
        g = ag_ref[:, lanes].astype(F32)
        cdf = 0.5 * (1.0 + jnp.tanh(math.sqrt(2.0 / math.pi) * (g + 0.044715 * (g * g * g))))
        h_sum = b_s[k, 0:n_tok, :] + b_s[nb + k, 0:n_tok, :]
        y_ref[:, lanes] = (h_sum * (g * cdf)).astype(y_ref.dtype)


def _lru(l, p, conv_w, conv_b, wr, br, wi, bi, lam, casts):
    assert LRU_BLOCK_W == LANES
    cw = LRU_NB * LANES
    col = lambda off: (lambda b, n: (b, off // cw + n))
    vec2 = pl.BlockSpec((None, 2, 1, cw), lambda b, n: (l, 0, 0, n))
    mat = pl.BlockSpec((None, 2, LRU_NB, LRU_BLOCK_W, LRU_BLOCK_W), lambda b, n: (l, 0, n, 0, 0))
    scan_buf = pltpu.VMEM((2 * LRU_NB, LRU_PAD_TOK, LANES), F32)
    return _call_with_casts(
        _lru_kernel,
        grid=(BATCH, LRU_WIDTH // cw),
        in_specs=[
            pl.BlockSpec((T_TOK, cw), col(P_AX)),
            pl.BlockSpec((T_TOK, cw), col(P_AG)),
            pl.BlockSpec((None, CONV_W, cw), lambda b, n: (l, 0, n)),
            pl.BlockSpec((None, 1, cw), lambda b, n: (l, 0, n)),
            mat, vec2, mat, vec2, vec2,
        ],
        out_spec=pl.BlockSpec((T_TOK, cw), lambda b, n: (b, n)),
        out_shape=jax.ShapeDtypeStruct((ROWS, LRU_WIDTH), BF16),
        scratch_shapes=[scan_buf, scan_buf, scan_buf, scan_buf],
        args=(p, p, conv_w, conv_b.reshape(DEPTH, 1, LRU_WIDTH), wr, br.reshape(DEPTH, 2, 1, LRU_WIDTH), wi,
              bi.reshape(DEPTH, 2, 1, LRU_WIDTH), lam.reshape(DEPTH, 2, 1, LRU_WIDTH)),
        casts=casts,
        name="rg_lru",
    )


def _mla_prep_kernel(bq_ref, bkv_ref, bkr_ref, qn_ref, wq_ref, kvn_ref, wk_ref, wv_ref, cos_ref, sin_ref,
                     q_out, k_out, v_out):
    cos = cos_ref[...]
    sin = sin_ref[...]
    qn = _rms_norm(bq_ref[...].astype(F32), qn_ref[...])
    q = jnp.dot(qn.astype(BF16), wq_ref[...], preferred_element_type=F32) * (MLA_SCALE * LOG2E)
    kvn = _rms_norm(bkv_ref[...].astype(F32), kvn_ref[...]).astype(BF16)
    kn = jnp.dot(kvn, wk_ref[...], preferred_element_type=F32).astype(BF16)
    v_out[...] = jnp.dot(kvn, wv_ref[...], preferred_element_type=F32).astype(BF16)
    lane = lax.broadcasted_iota(jnp.int32, (1, LANES), 1)
    kr = jnp.where(lane < MLA_ROPE, bkr_ref[...].astype(F32), 0.0)
    kr = _rope(kr, cos, sin).astype(BF16)
    for h in range(MLA_HEADS):
        lo = MLA_SLOT * h
        q_out[:, lo:lo + LANES] = q[:, lo:lo + LANES].astype(BF16)
        q_out[:, lo + LANES:lo + MLA_SLOT] = _rope(q[:, lo + LANES:lo + MLA_SLOT], cos, sin).astype(BF16)
        k_out[:, lo:lo + LANES] = kn[:, LANES * h:LANES * (h + 1)]
        k_out[:, lo + LANES:lo + MLA_SLOT] = kr


def _mla_prep(l, p, q_norm, wq_p, kv_norm, wk_p, wv_p, cos, sin):
    tm = TM_PREP
    tpb = T_TOK // tm
    full = lambda shape: pl.BlockSpec((None,) + shape, lambda i: (l,) + (0,) * len(shape))
    return pl.pallas_call(
        _mla_prep_kernel,
        grid=(ROWS // tm,),
        in_specs=[
            pl.BlockSpec((tm, MLA_Q_RANK), lambda i: (i, P_BQ // MLA_Q_RANK)),
            pl.BlockSpec((tm, MLA_KV_RANK), lambda i: (i, P_BKV // MLA_KV_RANK)),
            pl.BlockSpec((tm, LANES), lambda i: (i, P_BKR // LANES)),
            full((1, MLA_Q_RANK)),
            full((MLA_Q_RANK, MLA_HEADS * MLA_SLOT)),
            full((1, MLA_KV_RANK)),
            full((MLA_KV_RANK, MLA_HEADS * MLA_NOPE)),
            full((MLA_KV_RANK, MLA_HEADS * MLA_V)),
            pl.BlockSpec((tm, LANES), lambda i: (i % tpb, 0)),
            pl.BlockSpec((tm, LANES), lambda i: (i % tpb, 0)),
        ],
        out_specs=[
            pl.BlockSpec((tm, MLA_HEADS * MLA_SLOT), lambda i: (i, 0)),
            pl.BlockSpec((tm, MLA_HEADS * MLA_SLOT), lambda i: (i, 0)),
            pl.BlockSpec((tm, MLA_HEADS * MLA_V), lambda i: (i, 0)),
        ],
        out_shape=[
            jax.ShapeDtypeStruct((ROWS, MLA_HEADS * MLA_SLOT), BF16),
            jax.ShapeDtypeStruct((ROWS, MLA_HEADS * MLA_SLOT), BF16),
            jax.ShapeDtypeStruct((ROWS, MLA_HEADS * MLA_V), BF16),
        ],
        compiler_params=_params(("arbitrary",)),
        name="mla_prep",
    )(p, p, p, q_norm.reshape(DEPTH, 1, -1), wq_p, kv_norm.reshape(DEPTH, 1, -1), wk_p, wv_p, cos, sin)


def _mla_attn_kernel(q_ref, k_ref, v_ref, o_ref):
    tq = q_ref.shape[0]

    def attend(row0, n_rows, n_keys):
        for h in range(MLA_HEADS):
            q = q_ref[row0:row0 + n_rows, MLA_SLOT * h:MLA_SLOT * (h + 1)]
            k = k_ref[0:n_keys, MLA_SLOT * h:MLA_SLOT * (h + 1)]
            s = lax.dot_general(q, k, (((1,), (1,)), ((), ())), preferred_element_type=F32)
            m = jnp.max(s, -1, keepdims=True)
            p = jnp.exp2(s - m)
            l = jnp.sum(p, -1, keepdims=True)
            o = jnp.dot(p.astype(BF16), v_ref[0:n_keys, MLA_V * h:MLA_V * (h + 1)], preferred_element_type=F32)
            o_ref[row0:row0 + n_rows, MLA_V * h:MLA_V * (h + 1)] = (o / l).astype(o_ref.dtype)

    @pl.when(pl.program_id(1) == 0)
    def _():
        attend(0, CTX_LEN, CTX_LEN)
        attend(CTX_LEN, tq - CTX_LEN, T_TOK)

    @pl.when(pl.program_id(1) > 0)
    def _():
        attend(0, tq, T_TOK)


def _mla_attn(q, k, v):
    tq = TQ_MLA
    assert tq > CTX_LEN and T_TOK % tq == 0 and tq % (2 * SUBLANES) == 0
    tpb = T_TOK // tq
    per_batch = lambda width: pl.BlockSpec((T_TOK, width), lambda b, i: (b, 0), pipeline_mode=pl.Buffered(1))
    return pl.pallas_call(
        _mla_attn_kernel,
        grid=(BATCH, tpb),
        in_specs=[
            pl.BlockSpec((tq, MLA_HEADS * MLA_SLOT), lambda b, i: (b * tpb + i, 0)),
            per_batch(MLA_HEADS * MLA_SLOT),
            per_batch(MLA_HEADS * MLA_V),
        ],
        out_specs=pl.BlockSpec((tq, MLA_HEADS * MLA_V), lambda b, i: (b * tpb + i, 0)),
        out_shape=jax.ShapeDtypeStruct((ROWS, MLA_HEADS * MLA_V), BF16),
        compiler_params=_params(("arbitrary", "arbitrary")),
        name="mla_attn",
    )(q, k, v)


def _dup_half(x, upper):
    lane = lax.broadcasted_iota(jnp.int32, (1, LANES), 1)
    r = pltpu.roll(x, LANES // 2, 1)
    out = jnp.where(lane < LANES // 2, r, x) if upper else jnp.where(lane < LANES // 2, x, r)
    return out.astype(BF16)


def _swa_kernel(q_ref, k_ref, v_ref, cos_ref, sin_ref, sink_ref, o_ref, *, layer):
    tq = TQ_ATT
    band = tq + 2 * WINDOW
    qt = pl.program_id(1)
    q_row = pl.multiple_of(qt * tq, tq)
    lane = lax.broadcasted_iota(jnp.int32, (1, LANES), 1)
    low = lane < LANES // 2
    cos_q = cos_ref[pl.ds(q_row, tq), :]
    sin_q = sin_ref[pl.ds(q_row, tq), :]

    def q_block(c):
        x = q_ref[:, LANES * c:LANES * (c + 1)].astype(F32)
        return _rope(x, cos_q, sin_q) * (SWA_SCALE * LOG2E)

    def softmax_out(qm, sink, keys, vals, masks):
        ss = []
        for kk, mk in zip(keys, masks):
            s = lax.dot_general(qm, kk, (((1,), (1,)), ((), ())), preferred_element_type=F32)
            ss.append(s if mk is None else jnp.where(mk, s, -jnp.inf))
        m = jnp.maximum(functools.reduce(jnp.maximum, [jnp.max(s, -1, keepdims=True) for s in ss]), sink)
        ps = [jnp.exp2(s - m) for s in ss]
        l = functools.reduce(jnp.add, [jnp.sum(p, -1, keepdims=True) for p in ps]) + jnp.exp2(sink - m)
        o = functools.reduce(jnp.add, [jnp.dot(p.astype(BF16), vv, preferred_element_type=F32)
                                       for p, vv in zip(ps, vals)])
        return o / l

    def run(keys_of, vals_of, masks):
        for g in range(SWA_KV_HEADS):
            keys = keys_of(g)
            vals = vals_of(g)
            qms, sinks = [], []
            for c in (2 * g, 2 * g + 1):
                qc = q_block(c)
                qms += [jnp.where(low, qc, 0.0).astype(BF16), jnp.where(low, 0.0, qc).astype(BF16)]
                sinks += [sink_ref[layer, 2 * c] * LOG2E, sink_ref[layer, 2 * c + 1] * LOG2E]
            scores = [[lax.dot_general(qm, kk, (((1,), (1,)), ((), ())), preferred_element_type=F32) for qm in qms]
                      for kk in keys]
            ps, ls = [], []
            for h in range(SWA_REP):
                ss = [s[h] if mk is None else jnp.where(mk, s[h], -jnp.inf) for s, mk in zip(scores, masks)]
                m = jnp.maximum(functools.reduce(jnp.maximum, [jnp.max(s, -1, keepdims=True) for s in ss]), sinks[h])
                pr = [jnp.exp2(s - m) for s in ss]
                ls.append(functools.reduce(jnp.add, [jnp.sum(p, -1, keepdims=True) for p in pr]) + jnp.exp2(sinks[h] - m))
                ps.append([p.astype(BF16) for p in pr])
            outs = [[jnp.dot(ps[h][b], vv, preferred_element_type=F32) for h in range(SWA_REP)]
                    for b, vv in enumerate(vals)]
            os_ = [functools.reduce(jnp.add, [outs[b][h] for b in range(len(vals))]) / ls[h] for h in range(SWA_REP)]
            for i, c in enumerate((2 * g, 2 * g + 1)):
                o_ref[:, LANES * c:LANES * (c + 1)] = jnp.where(low, os_[2 * i], os_[2 * i + 1]).astype(o_ref.dtype)

    def ctx_kv(ref, g):
        return _dup_half(ref[0:CTX_LEN, LANES * (g // 2):LANES * (g // 2 + 1)].astype(F32), g % 2 == 1)

    @pl.when(qt == 0)
    def _():
        if layer == DEPTH - 1:
            o_ref[...] = jnp.zeros_like(o_ref)
        else:
            run(lambda g: [ctx_kv(k_ref, g)], lambda g: [ctx_kv(v_ref, g)], [None])

    @pl.when(qt > 0)
    def _():
        start = pl.multiple_of(jnp.minimum(q_row - WINDOW, T_TOK - band), LANES)
        qpos = q_row - CTX_LEN + lax.broadcasted_iota(jnp.int32, (tq, 1), 0)
        kpos = start - CTX_LEN + lax.broadcasted_iota(jnp.int32, (1, band), 1)
        valid = (jnp.abs(kpos - qpos) <= WINDOW) & (kpos >= 0)
        cos_k = cos_ref[pl.ds(start, band), :]
        sin_k = sin_ref[pl.ds(start, band), :]

        def band_k(g):
            kb = k_ref[pl.ds(start, band), LANES * (g // 2):LANES * (g // 2 + 1)].astype(F32)
            return _dup_half(_rope(kb, cos_k, sin_k), g % 2 == 1)

        def band_v(g):
            return _dup_half(v_ref[pl.ds(start, band), LANES * (g // 2):LANES * (g // 2 + 1)].astype(F32), g % 2 == 1)

        run(lambda g: [ctx_kv(k_ref, g), band_k(g)], lambda g: [ctx_kv(v_ref, g), band_v(g)], [None, valid])


def _swa(l, p, cos, sin, sinks, casts):
    tq = TQ_ATT
    tpb = T_TOK // tq
    kvw = SWA_KV_HEADS * SWA_HEAD_DIM
    return _call_with_casts(
        functools.partial(_swa_kernel, layer=l),
        grid=(BATCH, tpb),
        in_specs=[
            pl.BlockSpec((tq, SWA_HEADS * SWA_HEAD_DIM), lambda b, i: (b * tpb + i, P_CQ // (SWA_HEADS * SWA_HEAD_DIM))),
            pl.BlockSpec((T_TOK, kvw), lambda b, i: (b, P_CK // kvw)),
            pl.BlockSpec((T_TOK, kvw), lambda b, i: (b, P_CV // kvw)),
            pl.BlockSpec((T_TOK, LANES), lambda b, i: (0, 0)),
            pl.BlockSpec((T_TOK, LANES), lambda b, i: (0, 0)),
            pl.BlockSpec(memory_space=pltpu.SMEM),
        ],
        out_spec=pl.BlockSpec((tq, SWA_HEADS * SWA_HEAD_DIM), lambda b, i: (b * tpb + i, 0)),
        out_shape=jax.ShapeDtypeStruct((ROWS, SWA_HEADS * SWA_HEAD_DIM), BF16),
        args=(p, p, p, cos, sin, sinks),
        casts=casts,
        name="swa",
    )


def _gate_rows(i, tpb, tm, gvc_ref, gvb_ref, has_ctx):
    return _row_select(i, tpb, tm, gvc_ref[0, 0], gvb_ref[0, 0]) if has_ctx else gvb_ref[0, 0]


def _merge_kernel(ya_ref, yb_ref, yc_ref, ga_ref, gb_ref, gc_ref, wb_ref, wo_ref, x_ref, gvb_ref, gvc_ref,
                  lng_ref, lnb_ref, o_ref, *, tpb, tm, nj, has_ctx):
    i = pl.program_id(0)
    j = pl.program_id(1)

    @pl.when(j == 0)
    def _():
        o_ref[...] = jnp.zeros_like(o_ref)

    z = None
    for n, (y_ref, g_ref) in enumerate(((ya_ref, ga_ref), (yb_ref, gb_ref), (yc_ref, gc_ref))):
        zn = jnp.dot(y_ref[...], wb_ref[n], preferred_element_type=F32)
        zn = jax.nn.sigmoid(g_ref[...].astype(F32)) * zn
        z = zn if z is None else z + zn
    o_ref[...] += jnp.dot(z.astype(BF16), wo_ref[...], preferred_element_type=F32)

    @pl.when(j == nj - 1)
    def _():
        gate = _gate_rows(i, tpb, tm, gvc_ref, gvb_ref, has_ctx)
        o_ref[...] = _layer_norm(ALPHA * x_ref[...] + gate * o_ref[...], lng_ref[...], lnb_ref[...])


def _merge(l, ya, yb, yc, p, wb, wo, x, ada, ln_g, ln_b, latent_only, casts):
    tm = TM_MERGE_LATENT if latent_only else TM_MERGE
    tj = TJ_MERGE
    tpb = (SEQ if latent_only else T_TOK) // tm
    n_rows = BATCH * SEQ if latent_only else ROWS
    nj = D_MODEL // tj
    ybs = _row_spec(tm, BRANCH_W, lambda j: 0, latent_only)
    gate = lambda n: _row_spec(tm, tj, lambda j: (P_GATE + n * D_MODEL) // tj + j, latent_only)
    return _call_with_casts(
        functools.partial(_merge_kernel, tpb=tpb, tm=tm, nj=nj, has_ctx=not latent_only),
        grid=(n_rows // tm, nj),
        in_specs=[ybs, ybs, ybs, gate(0), gate(1), gate(2),
                  pl.BlockSpec((N_BRANCH, BRANCH_W, tj), lambda i, j: (0, 0, j)),
                  pl.BlockSpec((tj, D_MODEL), lambda i, j: (j, 0)),
                  _row_spec(tm, D_MODEL, lambda j: 0, latent_only)]
        + _vec_specs(l, 2, tpb) + [_layer_vec(l), _layer_vec(l)],
        out_spec=pl.BlockSpec((tm, D_MODEL), lambda i, j: (i, 0)),
        out_shape=jax.ShapeDtypeStruct((n_rows, D_MODEL), F32),
        args=(ya, yb, yc, p, p, p, wb, wo, x, ada, ada, ln_g.reshape(DEPTH, 1, -1), ln_b.reshape(DEPTH, 1, -1)),
        casts=casts,
        name="merge",
    )


def _ffn_kernel(x_ref, scb_ref, scc_ref, shb_ref, shc_ref, gvb_ref, gvc_ref, wg_ref, wu_ref, wo_ref,
                lng_ref, lnb_ref, o_ref, u_ref, *, tpb, tm, nj, has_ctx):
    i = pl.program_id(0)
    j = pl.program_id(1)

    @pl.when(j == 0)
    def _():
        sc = _gate_rows(i, tpb, tm, scc_ref, scb_ref, has_ctx)
        sh = _gate_rows(i, tpb, tm, shc_ref, shb_ref, has_ctx)
        u_ref[...] = (x_ref[...] * sc + sh).astype(BF16)
        o_ref[...] = jnp.zeros_like(o_ref)

    u = u_ref[...]
    gt = jnp.dot(u, wg_ref[...], preferred_element_type=F32)
    up = jnp.dot(u, wu_ref[...], preferred_element_type=F32)
    h = (gt * jax.nn.sigmoid(gt) * up).astype(BF16)
    o_ref[...] += jnp.dot(h, wo_ref[...], preferred_element_type=F32)

    @pl.when(j == nj - 1)
    def _():
        gate = _gate_rows(i, tpb, tm, gvc_ref, gvb_ref, has_ctx)
        o_ref[...] = _layer_norm(ALPHA * x_ref[...] + gate * o_ref[...], lng_ref[...], lnb_ref[...])


def _ffn(l, x, ada, w_in, w_out, ln_g, ln_b, latent_only, casts):
    tm = TM_FFN_LATENT if latent_only else TM_FFN
    tj = TJ_FFN
    tpb = (SEQ if latent_only else T_TOK) // tm
    n_rows = x.shape[0]
    nj = FFN_HIDDEN // tj
    row = pl.BlockSpec((tm, D_MODEL), lambda i, j: (i, 0))
    row_in = pl.BlockSpec((tm, D_MODEL), lambda i, j: (i, 0), pipeline_mode=pl.Buffered(1)) if latent_only else row
    return _call_with_casts(
        functools.partial(_ffn_kernel, tpb=tpb, tm=tm, nj=nj, has_ctx=not latent_only),
        grid=(n_rows // tm, nj),
        in_specs=[row_in] + _vec_specs(l, 4, tpb) + _vec_specs(l, 3, tpb) + _vec_specs(l, 5, tpb) + [
            pl.BlockSpec((D_MODEL, tj), lambda i, j: (0, j)),
            pl.BlockSpec((D_MODEL, tj), lambda i, j: (0, nj + j)),
            pl.BlockSpec((tj, D_MODEL), lambda i, j: (j, 0)),
            _layer_vec(l), _layer_vec(l)],
        out_spec=row,
        out_shape=jax.ShapeDtypeStruct((n_rows, D_MODEL), F32),
        scratch_shapes=[pltpu.VMEM((tm, D_MODEL), BF16)],
        args=(x, ada, ada, ada, ada, ada, ada, w_in, w_in, w_out, ln_g.reshape(DEPTH, 1, -1), ln_b.reshape(DEPTH, 1, -1)),
        casts=casts,
        name="ffn",
    )


def _rope_tables(both_halves):
    rot = SWA_HEAD_DIM
    half = rot // 2
    n_rows = SEQ // GRID_W
    row = jnp.repeat(jnp.arange(n_rows, dtype=F32), GRID_W)
    colp = jnp.tile(jnp.arange(GRID_W, dtype=F32), n_rows)
    inv = ROPE_BASE ** (-jnp.arange(0, half, 2, dtype=F32) / half)
    ang_r = row[:, None] * inv
    ang_c = colp[:, None] * inv
    cos = jnp.concatenate([jnp.cos(ang_r), jnp.cos(ang_r), jnp.cos(ang_c), jnp.cos(ang_c)], -1)
    sin = jnp.concatenate([-jnp.sin(ang_r), jnp.sin(ang_r), -jnp.sin(ang_c), jnp.sin(ang_c)], -1)
    if both_halves:
        cos = jnp.concatenate([cos, cos], -1)
        sin = jnp.concatenate([sin, sin], -1)
    else:
        cos = jnp.concatenate([cos, jnp.ones_like(cos)], -1)
        sin = jnp.concatenate([sin, jnp.zeros_like(sin)], -1)
    cos = jnp.concatenate([jnp.ones((CTX_LEN, LANES), F32), cos], 0)
    sin = jnp.concatenate([jnp.zeros((CTX_LEN, LANES), F32), sin], 0)
    return cos, sin


def _mla_weights(w_q_up, w_kv_up):
    wq = w_q_up.reshape(DEPTH, MLA_Q_RANK, MLA_HEADS, MLA_NOPE + MLA_ROPE)
    wq = jnp.pad(wq, ((0, 0), (0, 0), (0, 0), (0, MLA_SLOT - MLA_NOPE - MLA_ROPE)))
    wkv = w_kv_up.reshape(DEPTH, MLA_KV_RANK, MLA_HEADS, MLA_NOPE + MLA_V)
    wk = wkv[..., :MLA_NOPE].reshape(DEPTH, MLA_KV_RANK, MLA_HEADS * MLA_NOPE)
    wv = wkv[..., MLA_NOPE:].reshape(DEPTH, MLA_KV_RANK, MLA_HEADS * MLA_V)
    return wq.reshape(DEPTH, MLA_Q_RANK, MLA_HEADS * MLA_SLOT).astype(BF16), wk.astype(BF16), wv.astype(BF16)


def kernel(x, c, ctx, c_ctx, w_ada, b_ada, w_in, conv_w, conv_b, lru_wr, lru_br, lru_wi, lru_bi, lru_lambda, mla_q_norm, mla_w_q_up, mla_kv_norm, mla_w_kv_up, swa_sinks, w_branch, w_out, ln1_g, ln1_b, w_ffn_in, w_ffn_out, ln2_g, ln2_b):
    cos_mla, sin_mla = _rope_tables(False)
    cos_swa, sin_swa = _rope_tables(True)
    cvec = jnp.concatenate([c, c_ctx[None], jnp.zeros((SUBLANES - BATCH - 1, D_MODEL), F32)], 0)
    ada = _ada(cvec, w_ada, b_ada).reshape(DEPTH, 6, SUBLANES, 1, D_MODEL)
    w_in_t = jnp.transpose(w_in, (0, 2, 1))
    w_t = w_in_t[0].astype(BF16)
    w_branch_rows = w_branch.reshape(DEPTH, N_BRANCH * BRANCH_W, D_MODEL)
    wr, wi = lru_wr.astype(BF16), lru_wi.astype(BF16)
    wq_p, wk_p, wv_p = _mla_weights(mla_w_q_up, mla_w_kv_up)
    xs = jnp.concatenate([ctx, x], 1).reshape(ROWS, D_MODEL)
    for l in range(DEPTH):
        last = l == DEPTH - 1
        p, (wf_out,) = _mixin(l, xs, ada, w_t, [(w_ffn_out, l, 44)])
        ya, (wf_in,) = _lru(l, p, conv_w, conv_b, wr, lru_br, wi, lru_bi, lru_lambda, [(w_ffn_in, l, 16)])
        q, k, v = _mla_prep(l, p, mla_q_norm, wq_p, mla_kv_norm, wk_p, wv_p, cos_mla, sin_mla)
        yb = _mla_attn(q, k, v)
        yc, (wb, wo) = _swa(l, p, cos_swa, sin_swa, swa_sinks, [(w_branch_rows, l, 32), (w_out, l, 32)])
        x1, _ = _merge(l, ya, yb, yc, p, wb.reshape(N_BRANCH, BRANCH_W, D_MODEL), wo, xs, ada, ln1_g, ln1_b, last, [])
        xs, next_w = _ffn(l, x1, ada, wf_in, wf_out, ln2_g, ln2_b, last, [] if last else [(w_in_t, l + 1, 132)])
        if not last:
            w_t = next_w[0]
    return xs.reshape(BATCH, SEQ, D_MODEL)
```

```python
import functools
import math

import jax
import jax.numpy as jnp
from jax import lax
from jax.experimental import pallas as pl
from jax.experimental.pallas import tpu as pltpu

F32 = jnp.float32
BF16 = jnp.bfloat16

D_MODEL = 2048
BATCH = 4
SEQ = 2048
DEPTH = 2
GRID_W = 64
CTX_LEN = 256
LRU_WIDTH = 1024
LRU_BLOCKS = 8
LRU_BLOCK_W = LRU_WIDTH // LRU_BLOCKS
LRU_C = 8.0
CONV_W = 4
MLA_HEADS = 8
MLA_Q_RANK = 512
MLA_KV_RANK = 256
MLA_NOPE = 128
MLA_ROPE = 64
MLA_V = 128
MLA_SCALE = (MLA_NOPE + MLA_ROPE) ** -0.5
SWA_HEADS = 16
SWA_KV_HEADS = 4
SWA_HEAD_DIM = 64
SWA_REP = SWA_HEADS // SWA_KV_HEADS
SWA_SCALE = SWA_HEAD_DIM ** -0.5
WINDOW = 128
N_BRANCH = 3
BRANCH_W = 1024
MIX_IN = 4416
FFN_HIDDEN = -(-8 * D_MODEL // (3 * 256)) * 256
ROPE_BASE = 10000.0
LN_EPS = 1e-5
RMS_EPS = 1e-6
ALPHA = (2 * DEPTH) ** 0.25

T_TOK = CTX_LEN + SEQ
ROWS = BATCH * T_TOK
LANES = 128
SUBLANES = 8
MLA_SLOT = 2 * LANES

P_AX, P_AG, P_BQ, P_BKV, P_BKR = 0, 1024, 2048, 2560, 2816
P_CQ, P_CK, P_CV, P_GATE = 3072, 4096, 4352, 4608
P_COLS = P_GATE + N_BRANCH * D_MODEL

VMEM_LIMIT = 56 * 1024 * 1024

TM_MIX = 1152
TN_MIX = 1536
TM_MERGE = 576
TJ_MERGE = 512
TM_FFN = 768
TJ_FFN = 512
TM_PREP = 576
TQ_ATT = 256
TQ_MLA = 576
TN_ADA = 2048
TM_MERGE_LATENT = 512
TM_FFN_LATENT = 512
LOG2E = math.log2(math.e)


def _params(sem):
    return pltpu.CompilerParams(dimension_semantics=sem, vmem_limit_bytes=VMEM_LIMIT)


def _row_select(i, tiles_per_batch, tm, ctx_vec, batch_vec):
    rows = lax.broadcasted_iota(jnp.int32, (tm, 1), 0)
    n_ctx = jnp.where(i % tiles_per_batch == 0, CTX_LEN, 0)
    return jnp.where(rows < n_ctx, ctx_vec, batch_vec)


def _layer_norm(v, g, b):
    mu = jnp.mean(v, -1, keepdims=True)
    d = v - mu
    var = jnp.mean(d * d, -1, keepdims=True)
    return d * lax.rsqrt(var + LN_EPS) * g + b


def _rms_norm(v, g):
    return v * lax.rsqrt(jnp.mean(v * v, -1, keepdims=True) + RMS_EPS) * g


def _swap_pairs(x):
    lane = lax.broadcasted_iota(jnp.int32, (1, LANES), 1)
    first = (lane & 16) == 0
    return jnp.where(first, pltpu.roll(x, LANES - 16, 1), pltpu.roll(x, 16, 1))


def _rope(x, cos, sin):
    return x * cos + _swap_pairs(x) * sin


def _ada_kernel(c_ref, w_ref, b_ref, o_ref, *, chunks_per_vec):
    j = pl.program_id(1)
    c = c_ref[...]
    a = (c * jax.nn.sigmoid(c)).astype(BF16)
    acc = jnp.dot(a, w_ref[0].astype(BF16), preferred_element_type=F32)
    vec = j // chunks_per_vec
    one = jnp.where((vec == 1) | (vec == 4), 1.0, 0.0)
    o_ref[0, 0] = acc + b_ref[0] + one


def _ada(cvec, w_ada, b_ada):
    tn = TN_ADA
    cpv = D_MODEL // tn
    return pl.pallas_call(
        functools.partial(_ada_kernel, chunks_per_vec=cpv),
        grid=(DEPTH, 6 * cpv),
        in_specs=[
            pl.BlockSpec((SUBLANES, D_MODEL), lambda l, j: (0, 0)),
            pl.BlockSpec((1, D_MODEL, tn), lambda l, j: (l, 0, j)),
            pl.BlockSpec((1, 1, tn), lambda l, j: (l, 0, j)),
        ],
        out_specs=pl.BlockSpec((1, 1, SUBLANES, tn), lambda l, j: (l, j // cpv, 0, j % cpv)),
        out_shape=jax.ShapeDtypeStruct((DEPTH, 6, SUBLANES, D_MODEL), F32),
        compiler_params=_params(("arbitrary", "arbitrary")),
        name="ada",
    )(cvec, w_ada, b_ada.reshape(DEPTH, 1, 6 * D_MODEL))


def _vec_specs(l, vec, tiles_per_batch):
    return [
        pl.BlockSpec((None, 1, 1, 1, D_MODEL), lambda i, j: (l, vec, i // tiles_per_batch, 0, 0)),
        pl.BlockSpec((None, 1, 1, 1, D_MODEL), lambda i, j: (l, vec, BATCH, 0, 0)),
    ]


def _layer_vec(l):
    return pl.BlockSpec((None, 1, D_MODEL), lambda i, j: (l, 0, 0))


def _row_spec(tm, width, col_block, latent_only):
    if not latent_only:
        return pl.BlockSpec((tm, width), lambda i, j: (i, col_block(j)))
    tiles = SEQ // tm

    def index(i, j):
        row = (i // tiles) * T_TOK + CTX_LEN + (i % tiles) * tm
        return pl.multiple_of(row, CTX_LEN), pl.multiple_of(col_block(j) * width, LANES)

    return pl.BlockSpec((pl.Element(tm), pl.Element(width)), index)


def _call_with_casts(body, *, grid, in_specs, out_spec, out_shape, args, casts, name, scratch_shapes=()):
    n_in, n_side = len(in_specs), len(casts)
    n_steps = math.prod(grid)
    side_in, side_out, side_shape = [], [], []
    for src, layer, n_blocks in casts:
        _, n_rows, n_cols = src.shape
        rows = n_rows // n_blocks
        assert rows * n_blocks == n_rows and rows % (2 * SUBLANES) == 0 and n_blocks <= n_steps

        def block(*ids, n_blocks=n_blocks):
            step = ids[0] if len(ids) == 1 else ids[0] * grid[1] + ids[1]
            return jnp.minimum(step, n_blocks - 1)

        side_in.append(pl.BlockSpec((None, rows, n_cols), lambda *ids, block=block, layer=layer: (layer, block(*ids), 0)))
        side_out.append(pl.BlockSpec((rows, n_cols), lambda *ids, block=block: (block(*ids), 0)))
        side_shape.append(jax.ShapeDtypeStruct((n_rows, n_cols), BF16))

    def kernel(*refs):
        for src_ref, dst_ref in zip(refs[n_in:n_in + n_side], refs[n_in + n_side + 1:n_in + 2 * n_side + 1]):
            dst_ref[...] = src_ref[...].astype(dst_ref.dtype)
        body(*refs[:n_in], refs[n_in + n_side], *refs[n_in + 2 * n_side + 1:])

    out = pl.pallas_call(
        kernel,
        grid=grid,
        in_specs=list(in_specs) + side_in,
        out_specs=[out_spec] + side_out,
        out_shape=[out_shape] + side_shape,
        scratch_shapes=list(scratch_shapes),
        compiler_params=_params(("arbitrary",) * len(grid)),
        name=name,
    )(*args, *[c[0] for c in casts])
    return out[0], list(out[1:])


def _mixin_kernel(x_ref, scb_ref, scc_ref, shb_ref, shc_ref, w_ref, o_ref, u_ref, *, tpb, tm):
    i = pl.program_id(0)

    @pl.when(pl.program_id(1) == 0)
    def _():
        sc = _row_select(i, tpb, tm, scc_ref[0, 0], scb_ref[0, 0])
        sh = _row_select(i, tpb, tm, shc_ref[0, 0], shb_ref[0, 0])
        u_ref[...] = (x_ref[...] * sc + sh).astype(BF16)

    o_ref[...] = lax.dot_general(u_ref[...], w_ref[...], (((1,), (1,)), ((), ())),
                                 preferred_element_type=F32).astype(o_ref.dtype)


def _mixin(l, x, ada, w_t, casts):
    tm, tn = TM_MIX, TN_MIX
    tpb = T_TOK // tm
    pad = P_CQ - (P_BKR + MLA_ROPE)
    assert P_CQ % tn == 0 and w_t.shape[0] + pad == P_COLS

    def w_index(i, j):
        row = j * tn - jnp.where(j >= P_CQ // tn, pad, 0)
        return pl.multiple_of(row, MLA_ROPE), 0

    return _call_with_casts(
        functools.partial(_mixin_kernel, tpb=tpb, tm=tm),
        grid=(ROWS // tm, P_COLS // tn),
        in_specs=[pl.BlockSpec((tm, D_MODEL), lambda i, j: (i, 0))]
        + _vec_specs(l, 1, tpb)
        + _vec_specs(l, 0, tpb)
        + [pl.BlockSpec((pl.Element(tn), pl.Element(D_MODEL)), w_index)],
        out_spec=pl.BlockSpec((tm, tn), lambda i, j: (i, j)),
        out_shape=jax.ShapeDtypeStruct((ROWS, P_COLS), BF16),
        scratch_shapes=[pltpu.VMEM((tm, D_MODEL), BF16)],
        args=(x, ada, ada, ada, ada, w_t),
        casts=casts,
        name="mix_in",
    )


LRU_NB = 2
LRU_SEG_FWD = 292
LRU_SEG_BWD = 260
LRU_PAD_TOK = SUBLANES * LRU_SEG_FWD
assert LRU_PAD_TOK >= T_TOK and CTX_LEN + SUBLANES * LRU_SEG_BWD == LRU_PAD_TOK
assert LRU_SEG_FWD % 8 == 4 and LRU_SEG_BWD % 8 == 4


def _lru_kernel(ax_ref, ag_ref, cw_ref, cb_ref, wr_ref, br_ref, wi_ref, bi_ref, lam_ref, y_ref,
                a_s, b_s, h_s, p_s):
    n_tok = T_TOK
    nb = LRU_NB
    t = lax.broadcasted_iota(jnp.int32, (n_tok, 1), 0)
    is_ctx = t < CTX_LEN
    t_loc = jnp.where(is_ctx, t, t - CTX_LEN)
    seg_len = jnp.where(is_ctx, CTX_LEN, SEQ)

    for k in range(nb):
        lanes = slice(LANES * k, LANES * (k + 1))
        x = ax_ref[:, lanes].astype(F32)
        xm1 = jnp.where(t_loc >= 1, pltpu.roll(x, 1, 0), 0.0)
        xm2 = jnp.where(t_loc >= 2, pltpu.roll(x, 2, 0), 0.0)
        xp1 = jnp.where(t_loc <= seg_len - 2, pltpu.roll(x, n_tok - 1, 0), 0.0)
        w = cw_ref[:, lanes]
        xc = xm2 * w[0:1] + xm1 * w[1:2] + x * w[2:3] + xp1 * w[3:4] + cb_ref[:, lanes]
        xb = xc.astype(BF16)
        for d in range(2):
            i = d * nb + k
            tr = jnp.tanh(0.5 * (jnp.dot(xb, wr_ref[d, k], preferred_element_type=F32) + br_ref[d, :, lanes]))
            ti = jnp.tanh(0.5 * (jnp.dot(xb, wi_ref[d, k], preferred_element_type=F32) + bi_ref[d, :, lanes]))
            z = -lam_ref[d, :, lanes]
            softplus = jnp.maximum(z, 0.0) + jnp.log1p(jnp.exp(-jnp.abs(z)))
            half_c = (-0.5 * LRU_C) * softplus
            log_a = half_c * tr + half_c
            a = jnp.exp(log_a)
            v = jnp.tanh(-log_a) * (a * a + 1.0)
            mult = jnp.where(v > 0.0, v * lax.rsqrt(v), 0.0)
            a_s[i, 0:n_tok, :] = a
            b_s[i, 0:n_tok, :] = (0.5 * mult) * (ti + 1.0) * xc
            a_s[i, n_tok:LRU_PAD_TOK, :] = jnp.ones((LRU_PAD_TOK - n_tok, LANES), F32)
            b_s[i, n_tok:LRU_PAD_TOK, :] = jnp.zeros((LRU_PAD_TOK - n_tok, LANES), F32)

    sub = lax.broadcasted_iota(jnp.int32, (CTX_LEN, 1), 0) & (SUBLANES - 1)
    for k in range(nb):
        a = a_s[nb + k, 0:CTX_LEN, :]
        bb = b_s[nb + k, 0:CTX_LEN, :]
        for s in (1, 2, 4):
            m = sub < SUBLANES - s
            a_sh = pltpu.roll(a, CTX_LEN - s, 0)
            b_sh = pltpu.roll(bb, CTX_LEN - s, 0)
            bb = jnp.where(m, bb + a * b_sh, bb)
            a = jnp.where(m, a * a_sh, a)
        a_s[nb + k, 0:CTX_LEN, :] = a
        b_s[nb + k, 0:CTX_LEN, :] = bb

    def ctx_bwd(j, hs):
        rows = pl.ds(pl.multiple_of((CTX_LEN // SUBLANES - 1 - j) * SUBLANES, SUBLANES), SUBLANES)
        out = []
        for k in range(nb):
            hg = b_s[nb + k, rows, :] + a_s[nb + k, rows, :] * hs[k]
            b_s[nb + k, rows, :] = hg
            out.append(hg[0:1])
        return tuple(out)

    h_ctx0 = lax.fori_loop(0, CTX_LEN // SUBLANES, ctx_bwd, (jnp.zeros((1, LANES), F32),) * nb, unroll=4)

    def seg_rows(base, s, seg):
        return pl.ds(base + s, SUBLANES, stride=seg)

    def scan_step(i, rows, h, p):
        av = a_s[i, rows, :]
        h = av * h + b_s[i, rows, :]
        p = av * p
        h_s[i, rows, :] = h
        p_s[i, rows, :] = p
        return h, p

    def fwd_steps(s, c):
        rows = seg_rows(0, s, LRU_SEG_FWD)
        return sum((scan_step(k, rows, c[2 * k], c[2 * k + 1]) for k in range(nb)), ())

    def bwd_steps(s, c):
        rows = seg_rows(CTX_LEN, LRU_SEG_BWD - 1 - s, LRU_SEG_BWD)
        return sum((scan_step(nb + k, rows, c[2 * k], c[2 * k + 1]) for k in range(nb)), ())

    zeros8 = jnp.zeros((SUBLANES, LANES), F32)
    ones8 = jnp.ones((SUBLANES, LANES), F32)
    init = (zeros8, ones8) * nb

    def both(s, c):
        return fwd_steps(s, c[:2 * nb]) + bwd_steps(s, c[2 * nb:])

    c = lax.fori_loop(0, LRU_SEG_BWD, both, init + init, unroll=2)
    cf_state = lax.fori_loop(LRU_SEG_BWD, LRU_SEG_FWD, fwd_steps, c[:2 * nb], unroll=2)
    cb_state = c[2 * nb:]

    seg = lax.broadcasted_iota(jnp.int32, (SUBLANES, 1), 0)
    cfs, cbs = [], []
    for k in range(nb):
        hf, pf = cf_state[2 * k], cf_state[2 * k + 1]
        hb, pb = cb_state[2 * k], cb_state[2 * k + 1]
        cf = zeros8
        cb = jnp.where(seg == SUBLANES - 1, h_ctx0[k], 0.0)
        for _ in range(SUBLANES - 1):
            cf = jnp.where(seg >= 1, pltpu.roll(hf + pf * cf, 1, 0), 0.0)
            cb = jnp.where(seg <= SUBLANES - 2, pltpu.roll(hb + pb * cb, SUBLANES - 1, 0), h_ctx0[k])
        cfs.append(cf)
        cbs.append(cb)

    def fix_fwd(s):
        rows = seg_rows(0, s, LRU_SEG_FWD)
        for k in range(nb):
            b_s[k, rows, :] = h_s[k, rows, :] + p_s[k, rows, :] * cfs[k]

    def fix_bwd(s):
        rows = seg_rows(CTX_LEN, s, LRU_SEG_BWD)
        for k in range(nb):
            b_s[nb + k, rows, :] = h_s[nb + k, rows, :] + p_s[nb + k, rows, :] * cbs[k]

    def fix_both(s, c):
        fix_fwd(s)
        fix_bwd(s)
        return c

    def fix_tail(s, c):
        fix_fwd(s)
        return c

    lax.fori_loop(0, LRU_SEG_BWD, fix_both, 0, unroll=2)
    lax.fori_loop(LRU_SEG_BWD, LRU_SEG_FWD, fix_tail, 0, unroll=2)

    for k in range(nb):
        lanes = slice(LANES * k, LANES * (k + 1))
        g = ag_ref[:, lanes].astype(F32)
        cdf = 0.5 * (1.0 + jnp.tanh(math.sqrt(2.0 / math.pi) * (g + 0.044715 * (g * g * g))))
        h_sum = b_s[k, 0:n_tok, :] + b_s[nb + k, 0:n_tok, :]
        y_ref[:, lanes] = (h_sum * (g * cdf)).astype(y_ref.dtype)


def _lru(l, p, conv_w, conv_b, wr, br, wi, bi, lam, casts):
    assert LRU_BLOCK_W == LANES
    cw = LRU_NB * LANES
    col = lambda off: (lambda b, n: (b, off // cw + n))
    vec2 = pl.BlockSpec((None, 2, 1, cw), lambda b, n: (l, 0, 0, n))
    mat = pl.BlockSpec((None, 2, LRU_NB, LRU_BLOCK_W, LRU_BLOCK_W), lambda b, n: (l, 0, n, 0, 0))
    scan_buf = pltpu.VMEM((2 * LRU_NB, LRU_PAD_TOK, LANES), F32)
    return _call_with_casts(
        _lru_kernel,
        grid=(BATCH, LRU_WIDTH // cw),
        in_specs=[
            pl.BlockSpec((T_TOK, cw), col(P_AX)),
            pl.BlockSpec((T_TOK, cw), col(P_AG)),
            pl.BlockSpec((None, CONV_W, cw), lambda b, n: (l, 0, n)),
            pl.BlockSpec((None, 1, cw), lambda b, n: (l, 0, n)),
            mat, vec2, mat, vec2, vec2,
        ],
        out_spec=pl.BlockSpec((T_TOK, cw), lambda b, n: (b, n)),
        out_shape=jax.ShapeDtypeStruct((ROWS, LRU_WIDTH), BF16),
        scratch_shapes=[scan_buf, scan_buf, scan_buf, scan_buf],
        args=(p, p, conv_w, conv_b.reshape(DEPTH, 1, LRU_WIDTH), wr, br.reshape(DEPTH, 2, 1, LRU_WIDTH), wi,
              bi.reshape(DEPTH, 2, 1, LRU_WIDTH), lam.reshape(DEPTH, 2, 1, LRU_WIDTH)),
        casts=casts,
        name="rg_lru",
    )


def _mla_prep_kernel(bq_ref, bkv_ref, bkr_ref, qn_ref, wq_ref, kvn_ref, wk_ref, wv_ref, cos_ref, sin_ref,
                     q_out, k_out, v_out):
    cos = cos_ref[...]
    sin = sin_ref[...]
    qn = _rms_norm(bq_ref[...].astype(F32), qn_ref[...])
    q = jnp.dot(qn.astype(BF16), wq_ref[...], preferred_element_type=F32) * (MLA_SCALE * LOG2E)
    kvn = _rms_norm(bkv_ref[...].astype(F32), kvn_ref[...]).astype(BF16)
    kn = jnp.dot(kvn, wk_ref[...], preferred_element_type=F32).astype(BF16)
    v_out[...] = jnp.dot(kvn, wv_ref[...], preferred_element_type=F32).astype(BF16)
    lane = lax.broadcasted_iota(jnp.int32, (1, LANES), 1)
    kr = jnp.where(lane < MLA_ROPE, bkr_ref[...].astype(F32), 0.0)
    kr = _rope(kr, cos, sin).astype(BF16)
    for h in range(MLA_HEADS):
        lo = MLA_SLOT * h
        q_out[:, lo:lo + LANES] = q[:, lo:lo + LANES].astype(BF16)
        q_out[:, lo + LANES:lo + MLA_SLOT] = _rope(q[:, lo + LANES:lo + MLA_SLOT], cos, sin).astype(BF16)
        k_out[:, lo:lo + LANES] = kn[:, LANES * h:LANES * (h + 1)]
        k_out[:, lo + LANES:lo + MLA_SLOT] = kr


def _mla_prep(l, p, q_norm, wq_p, kv_norm, wk_p, wv_p, cos, sin):
    tm = TM_PREP
    tpb = T_TOK // tm
    full = lambda shape: pl.BlockSpec((None,) + shape, lambda i: (l,) + (0,) * len(shape))
    return pl.pallas_call(
        _mla_prep_kernel,
        grid=(ROWS // tm,),
        in_specs=[
            pl.BlockSpec((tm, MLA_Q_RANK), lambda i: (i, P_BQ // MLA_Q_RANK)),
            pl.BlockSpec((tm, MLA_KV_RANK), lambda i: (i, P_BKV // MLA_KV_RANK)),
            pl.BlockSpec((tm, LANES), lambda i: (i, P_BKR // LANES)),
            full((1, MLA_Q_RANK)),
            full((MLA_Q_RANK, MLA_HEADS * MLA_SLOT)),
            full((1, MLA_KV_RANK)),
            full((MLA_KV_RANK, MLA_HEADS * MLA_NOPE)),
            full((MLA_KV_RANK, MLA_HEADS * MLA_V)),
            pl.BlockSpec((tm, LANES), lambda i: (i % tpb, 0)),
            pl.BlockSpec((tm, LANES), lambda i: (i % tpb, 0)),
        ],
        out_specs=[
            pl.BlockSpec((tm, MLA_HEADS * MLA_SLOT), lambda i: (i, 0)),
            pl.BlockSpec((tm, MLA_HEADS * MLA_SLOT), lambda i: (i, 0)),
            pl.BlockSpec((tm, MLA_HEADS * MLA_V), lambda i: (i, 0)),
        ],
        out_shape=[
            jax.ShapeDtypeStruct((ROWS, MLA_HEADS * MLA_SLOT), BF16),
            jax.ShapeDtypeStruct((ROWS, MLA_HEADS * MLA_SLOT), BF16),
            jax.ShapeDtypeStruct((ROWS, MLA_HEADS * MLA_V), BF16),
        ],
        compiler_params=_params(("arbitrary",)),
        name="mla_prep",
    )(p, p, p, q_norm.reshape(DEPTH, 1, -1), wq_p, kv_norm.reshape(DEPTH, 1, -1), wk_p, wv_p, cos, sin)


def _mla_attn_kernel(q_ref, k_ref, v_ref, o_ref):
    tq = q_ref.shape[0]

    def attend(row0, n_rows, n_keys):
        for h in range(MLA_HEADS):
            q = q_ref[row0:row0 + n_rows, MLA_SLOT * h:MLA_SLOT * (h + 1)]
            k = k_ref[0:n_keys, MLA_SLOT * h:MLA_SLOT * (h + 1)]
            s = lax.dot_general(q, k, (((1,), (1,)), ((), ())), preferred_element_type=F32)
            m = jnp.max(s, -1, keepdims=True)
            p = jnp.exp2(s - m)
            l = jnp.sum(p, -1, keepdims=True)
            o = jnp.dot(p.astype(BF16), v_ref[0:n_keys, MLA_V * h:MLA_V * (h + 1)], preferred_element_type=F32)
            o_ref[row0:row0 + n_rows, MLA_V * h:MLA_V * (h + 1)] = (o / l).astype(o_ref.dtype)

    @pl.when(pl.program_id(1) == 0)
    def _():
        attend(0, CTX_LEN, CTX_LEN)
        attend(CTX_LEN, tq - CTX_LEN, T_TOK)

    @pl.when(pl.program_id(1) > 0)
    def _():
        attend(0, tq, T_TOK)


def _mla_attn(q, k, v):
    tq = TQ_MLA
    assert tq > CTX_LEN and T_TOK % tq == 0 and tq % (2 * SUBLANES) == 0
    tpb = T_TOK // tq
    per_batch = lambda width: pl.BlockSpec((T_TOK, width), lambda b, i: (b, 0), pipeline_mode=pl.Buffered(1))
    return pl.pallas_call(
        _mla_attn_kernel,
        grid=(BATCH, tpb),
        in_specs=[
            pl.BlockSpec((tq, MLA_HEADS * MLA_SLOT), lambda b, i: (b * tpb + i, 0)),
            per_batch(MLA_HEADS * MLA_SLOT),
            per_batch(MLA_HEADS * MLA_V),
        ],
        out_specs=pl.BlockSpec((tq, MLA_HEADS * MLA_V), lambda b, i: (b * tpb + i, 0)),
        out_shape=jax.ShapeDtypeStruct((ROWS, MLA_HEADS * MLA_V), BF16),
        compiler_params=_params(("arbitrary", "arbitrary")),
        name="mla_attn",
    )(q, k, v)


def _dup_half(x, upper):
    lane = lax.broadcasted_iota(jnp.int32, (1, LANES), 1)
    r = pltpu.roll(x, LANES // 2, 1)
    out = jnp.where(lane < LANES // 2, r, x) if upper else jnp.where(lane < LANES // 2, x, r)
    return out.astype(BF16)


def _swa_kernel(q_ref, k_ref, v_ref, cos_ref, sin_ref, sink_ref, o_ref, *, layer):
    tq = TQ_ATT
    band = tq + 2 * WINDOW
    qt = pl.program_id(1)
    q_row = pl.multiple_of(qt * tq, tq)
    lane = lax.broadcasted_iota(jnp.int32, (1, LANES), 1)
    low = lane < LANES // 2
    cos_q = cos_ref[pl.ds(q_row, tq), :]
    sin_q = sin_ref[pl.ds(q_row, tq), :]

    def q_block(c):
        x = q_ref[:, LANES * c:LANES * (c + 1)].astype(F32)
        return _rope(x, cos_q, sin_q) * (SWA_SCALE * LOG2E)

    def softmax_out(qm, sink, keys, vals, masks):
        ss = []
        for kk, mk in zip(keys, masks):
            s = lax.dot_general(qm, kk, (((1,), (1,)), ((), ())), preferred_element_type=F32)
            ss.append(s if mk is None else jnp.where(mk, s, -jnp.inf))
        m = jnp.maximum(functools.reduce(jnp.maximum, [jnp.max(s, -1, keepdims=True) for s in ss]), sink)
        ps = [jnp.exp2(s - m) for s in ss]
        l = functools.reduce(jnp.add, [jnp.sum(p, -1, keepdims=True) for p in ps]) + jnp.exp2(sink - m)
        o = functools.reduce(jnp.add, [jnp.dot(p.astype(BF16), vv, preferred_element_type=F32)
                                       for p, vv in zip(ps, vals)])
        return o / l

    def run(keys_of, vals_of, masks):
        for g in range(SWA_KV_HEADS):
            keys = keys_of(g)
            vals = vals_of(g)
            qms, sinks = [], []
            for c in (2 * g, 2 * g + 1):
                qc = q_block(c)
                qms += [jnp.where(low, qc, 0.0).astype(BF16), jnp.where(low, 0.0, qc).astype(BF16)]
                sinks += [sink_ref[layer, 2 * c] * LOG2E, sink_ref[layer, 2 * c + 1] * LOG2E]
            scores = [[lax.dot_general(qm, kk, (((1,), (1,)), ((), ())), preferred_element_type=F32) for qm in qms]
                      for kk in keys]
            ps, ls = [], []
            for h in range(SWA_REP):
                ss = [s[h] if mk is None else jnp.where(mk, s[h], -jnp.inf) for s, mk in zip(scores, masks)]
                m = jnp.maximum(functools.reduce(jnp.maximum, [jnp.max(s, -1, keepdims=True) for s in ss]), sinks[h])
                pr = [jnp.exp2(s - m) for s in ss]
                ls.append(functools.reduce(jnp.add, [jnp.sum(p, -1, keepdims=True) for p in pr]) + jnp.exp2(sinks[h] - m))
                ps.append([p.astype(BF16) for p in pr])
            outs = [[jnp.dot(ps[h][b], vv, preferred_element_type=F32) for h in range(SWA_REP)]
                    for b, vv in enumerate(vals)]
            os_ = [functools.reduce(jnp.add, [outs[b][h] for b in range(len(vals))]) / ls[h] for h in range(SWA_REP)]
            for i, c in enumerate((2 * g, 2 * g + 1)):
                o_ref[:, LANES * c:LANES * (c + 1)] = jnp.where(low, os_[2 * i], os_[2 * i + 1]).astype(o_ref.dtype)

    def ctx_kv(ref, g):
        return _dup_half(ref[0:CTX_LEN, LANES * (g // 2):LANES * (g // 2 + 1)].astype(F32), g % 2 == 1)

    @pl.when(qt == 0)
    def _():
        if layer == DEPTH - 1:
            o_ref[...] = jnp.zeros_like(o_ref)
        else:
            run(lambda g: [ctx_kv(k_ref, g)], lambda g: [ctx_kv(v_ref, g)], [None])

    @pl.when(qt > 0)
    def _():
        start = pl.multiple_of(jnp.minimum(q_row - WINDOW, T_TOK - band), LANES)
        qpos = q_row - CTX_LEN + lax.broadcasted_iota(jnp.int32, (tq, 1), 0)
        kpos = start - CTX_LEN + lax.broadcasted_iota(jnp.int32, (1, band), 1)
        valid = (jnp.abs(kpos - qpos) <= WINDOW) & (kpos >= 0)
        cos_k = cos_ref[pl.ds(start, band), :]
        sin_k = sin_ref[pl.ds(start, band), :]

        def band_k(g):
            kb = k_ref[pl.ds(start, band), LANES * (g // 2):LANES * (g // 2 + 1)].astype(F32)
            return _dup_half(_rope(kb, cos_k, sin_k), g % 2 == 1)

        def band_v(g):
            return _dup_half(v_ref[pl.ds(start, band), LANES * (g // 2):LANES * (g // 2 + 1)].astype(F32), g % 2 == 1)

        run(lambda g: [ctx_kv(k_ref, g), band_k(g)], lambda g: [ctx_kv(v_ref, g), band_v(g)], [None, valid])


def _swa(l, p, cos, sin, sinks, casts):
    tq = TQ_ATT
    tpb = T_TOK // tq
    kvw = SWA_KV_HEADS * SWA_HEAD_DIM
    return _call_with_casts(
        functools.partial(_swa_kernel, layer=l),
        grid=(BATCH, tpb),
        in_specs=[
            pl.BlockSpec((tq, SWA_HEADS * SWA_HEAD_DIM), lambda b, i: (b * tpb + i, P_CQ // (SWA_HEADS * SWA_HEAD_DIM))),
            pl.BlockSpec((T_TOK, kvw), lambda b, i: (b, P_CK // kvw)),
            pl.BlockSpec((T_TOK, kvw), lambda b, i: (b, P_CV // kvw)),
            pl.BlockSpec((T_TOK, LANES), lambda b, i: (0, 0)),
            pl.BlockSpec((T_TOK, LANES), lambda b, i: (0, 0)),
            pl.BlockSpec(memory_space=pltpu.SMEM),
        ],
        out_spec=pl.BlockSpec((tq, SWA_HEADS * SWA_HEAD_DIM), lambda b, i: (b * tpb + i, 0)),
        out_shape=jax.ShapeDtypeStruct((ROWS, SWA_HEADS * SWA_HEAD_DIM), BF16),
        args=(p, p, p, cos, sin, sinks),
        casts=casts,
        name="swa",
    )


def _gate_rows(i, tpb, tm, gvc_ref, gvb_ref, has_ctx):
    return _row_select(i, tpb, tm, gvc_ref[0, 0], gvb_ref[0, 0]) if has_ctx else gvb_ref[0, 0]


def _merge_kernel(ya_ref, yb_ref, yc_ref, ga_ref, gb_ref, gc_ref, wb_ref, wo_ref, x_ref, gvb_ref, gvc_ref,
                  lng_ref, lnb_ref, o_ref, *, tpb, tm, nj, has_ctx):
    i = pl.program_id(0)
    j = pl.program_id(1)

    @pl.when(j == 0)
    def _():
        o_ref[...] = jnp.zeros_like(o_ref)

    z = None
    for n, (y_ref, g_ref) in enumerate(((ya_ref, ga_ref), (yb_ref, gb_ref), (yc_ref, gc_ref))):
        zn = jnp.dot(y_ref[...], wb_ref[n], preferred_element_type=F32)
        zn = jax.nn.sigmoid(g_ref[...].astype(F32)) * zn
        z = zn if z is None else z + zn
    o_ref[...] += jnp.dot(z.astype(BF16), wo_ref[...], preferred_element_type=F32)

    @pl.when(j == nj - 1)
    def _():
        gate = _gate_rows(i, tpb, tm, gvc_ref, gvb_ref, has_ctx)
        o_ref[...] = _layer_norm(ALPHA * x_ref[...] + gate * o_ref[...], lng_ref[...], lnb_ref[...])


def _merge(l, ya, yb, yc, p, wb, wo, x, ada, ln_g, ln_b, latent_only, casts):
    tm = TM_MERGE_LATENT if latent_only else TM_MERGE
    tj = TJ_MERGE
    tpb = (SEQ if latent_only else T_TOK) // tm
    n_rows = BATCH * SEQ if latent_only else ROWS
    nj = D_MODEL // tj
    ybs = _row_spec(tm, BRANCH_W, lambda j: 0, latent_only)
    gate = lambda n: _row_spec(tm, tj, lambda j: (P_GATE + n * D_MODEL) // tj + j, latent_only)
    return _call_with_casts(
        functools.partial(_merge_kernel, tpb=tpb, tm=tm, nj=nj, has_ctx=not latent_only),
        grid=(n_rows // tm, nj),
        in_specs=[ybs, ybs, ybs, gate(0), gate(1), gate(2),
                  pl.BlockSpec((N_BRANCH, BRANCH_W, tj), lambda i, j: (0, 0, j)),
                  pl.BlockSpec((tj, D_MODEL), lambda i, j: (j, 0)),
                  _row_spec(tm, D_MODEL, lambda j: 0, latent_only)]
        + _vec_specs(l, 2, tpb) + [_layer_vec(l), _layer_vec(l)],
        out_spec=pl.BlockSpec((tm, D_MODEL), lambda i, j: (i, 0)),
        out_shape=jax.ShapeDtypeStruct((n_rows, D_MODEL), F32),
        args=(ya, yb, yc, p, p, p, wb, wo, x, ada, ada, ln_g.reshape(DEPTH, 1, -1), ln_b.reshape(DEPTH, 1, -1)),
        casts=casts,
        name="merge",
    )


def _ffn_kernel(x_ref, scb_ref, scc_ref, shb_ref, shc_ref, gvb_ref, gvc_ref, wg_ref, wu_ref, wo_ref,
                lng_ref, lnb_ref, o_ref, u_ref, *, tpb, tm, nj, has_ctx):
    i = pl.program_id(0)
    j = pl.program_id(1)

    @pl.when(j == 0)
    def _():
        sc = _gate_rows(i, tpb, tm, scc_ref, scb_ref, has_ctx)
        sh = _gate_rows(i, tpb, tm, shc_ref, shb_ref, has_ctx)
        u_ref[...] = (x_ref[...] * sc + sh).astype(BF16)
        o_ref[...] = jnp.zeros_like(o_ref)

    u = u_ref[...]
    gt = jnp.dot(u, wg_ref[...], preferred_element_type=F32)
    up = jnp.dot(u, wu_ref[...], preferred_element_type=F32)
    h = (gt * jax.nn.sigmoid(gt) * up).astype(BF16)
    o_ref[...] += jnp.dot(h, wo_ref[...], preferred_element_type=F32)

    @pl.when(j == nj - 1)
    def _():
        gate = _gate_rows(i, tpb, tm, gvc_ref, gvb_ref, has_ctx)
        o_ref[...] = _layer_norm(ALPHA * x_ref[...] + gate * o_ref[...], lng_ref[...], lnb_ref[...])


def _ffn(l, x, ada, w_in, w_out, ln_g, ln_b, latent_only, casts):
    tm = TM_FFN_LATENT if latent_only else TM_FFN
    tj = TJ_FFN
    tpb = (SEQ if latent_only else T_TOK) // tm
    n_rows = x.shape[0]
    nj = FFN_HIDDEN // tj
    row = pl.BlockSpec((tm, D_MODEL), lambda i, j: (i, 0))
    row_in = row
    return _call_with_casts(
        functools.partial(_ffn_kernel, tpb=tpb, tm=tm, nj=nj, has_ctx=not latent_only),
        grid=(n_rows // tm, nj),
        in_specs=[row_in] + _vec_specs(l, 4, tpb) + _vec_specs(l, 3, tpb) + _vec_specs(l, 5, tpb) + [
            pl.BlockSpec((D_MODEL, tj), lambda i, j: (0, j)),
            pl.BlockSpec((D_MODEL, tj), lambda i, j: (0, nj + j)),
            pl.BlockSpec((tj, D_MODEL), lambda i, j: (j, 0)),
            _layer_vec(l), _layer_vec(l)],
        out_spec=row,
        out_shape=jax.ShapeDtypeStruct((n_rows, D_MODEL), F32),
        scratch_shapes=[pltpu.VMEM((tm, D_MODEL), BF16)],
        args=(x, ada, ada, ada, ada, ada, ada, w_in, w_in, w_out, ln_g.reshape(DEPTH, 1, -1), ln_b.reshape(DEPTH, 1, -1)),
        casts=casts,
        name="ffn",
    )


def _rope_tables(both_halves):
    rot = SWA_HEAD_DIM
    half = rot // 2
    n_rows = SEQ // GRID_W
    row = jnp.repeat(jnp.arange(n_rows, dtype=F32), GRID_W)
    colp = jnp.tile(jnp.arange(GRID_W, dtype=F32), n_rows)
    inv = ROPE_BASE ** (-jnp.arange(0, half, 2, dtype=F32) / half)
    ang_r = row[:, None] * inv
    ang_c = colp[:, None] * inv
    cos = jnp.concatenate([jnp.cos(ang_r), jnp.cos(ang_r), jnp.cos(ang_c), jnp.cos(ang_c)], -1)
    sin = jnp.concatenate([-jnp.sin(ang_r), jnp.sin(ang_r), -jnp.sin(ang_c), jnp.sin(ang_c)], -1)
    if both_halves:
        cos = jnp.concatenate([cos, cos], -1)
        sin = jnp.concatenate([sin, sin], -1)
    else:
        cos = jnp.concatenate([cos, jnp.ones_like(cos)], -1)
        sin = jnp.concatenate([sin, jnp.zeros_like(sin)], -1)
    cos = jnp.concatenate([jnp.ones((CTX_LEN, LANES), F32), cos], 0)
    sin = jnp.concatenate([jnp.zeros((CTX_LEN, LANES), F32), sin], 0)
    return cos, sin


def _mla_weights(w_q_up, w_kv_up):
    wq = w_q_up.reshape(DEPTH, MLA_Q_RANK, MLA_HEADS, MLA_NOPE + MLA_ROPE)
    wq = jnp.pad(wq, ((0, 0), (0, 0), (0, 0), (0, MLA_SLOT - MLA_NOPE - MLA_ROPE)))
    wkv = w_kv_up.reshape(DEPTH, MLA_KV_RANK, MLA_HEADS, MLA_NOPE + MLA_V)
    wk = wkv[..., :MLA_NOPE].reshape(DEPTH, MLA_KV_RANK, MLA_HEADS * MLA_NOPE)
    wv = wkv[..., MLA_NOPE:].reshape(DEPTH, MLA_KV_RANK, MLA_HEADS * MLA_V)
    return wq.reshape(DEPTH, MLA_Q_RANK, MLA_HEADS * MLA_SLOT).astype(BF16), wk.astype(BF16), wv.astype(BF16)


def kernel(x, c, ctx, c_ctx, w_ada, b_ada, w_in, conv_w, conv_b, lru_wr, lru_br, lru_wi, lru_bi, lru_lambda, mla_q_norm, mla_w_q_up, mla_kv_norm, mla_w_kv_up, swa_sinks, w_branch, w_out, ln1_g, ln1_b, w_ffn_in, w_ffn_out, ln2_g, ln2_b):
    cos_mla, sin_mla = _rope_tables(False)
    cos_swa, sin_swa = _rope_tables(True)
    cvec = jnp.concatenate([c, c_ctx[None], jnp.zeros((SUBLANES - BATCH - 1, D_MODEL), F32)], 0)
    ada = _ada(cvec, w_ada, b_ada).reshape(DEPTH, 6, SUBLANES, 1, D_MODEL)
    w_in_t = jnp.transpose(w_in, (0, 2, 1))
    w_t = w_in_t[0].astype(BF16)
    w_branch_rows = w_branch.reshape(DEPTH, N_BRANCH * BRANCH_W, D_MODEL)
    wr, wi = lru_wr.astype(BF16), lru_wi.astype(BF16)
    wq_p, wk_p, wv_p = _mla_weights(mla_w_q_up, mla_w_kv_up)
    xs = jnp.concatenate([ctx, x], 1).reshape(ROWS, D_MODEL)
    for l in range(DEPTH):
        last = l == DEPTH - 1
        p, (wf_out,) = _mixin(l, xs, ada, w_t, [(w_ffn_out, l, 44)])
        ya, (wf_in,) = _lru(l, p, conv_w, conv_b, wr, lru_br, wi, lru_bi, lru_lambda, [(w_ffn_in, l, 16)])
        q, k, v = _mla_prep(l, p, mla_q_norm, wq_p, mla_kv_norm, wk_p, wv_p, cos_mla, sin_mla)
        yb = _mla_attn(q, k, v)
        yc, (wb, wo) = _swa(l, p, cos_swa, sin_swa, swa_sinks, [(w_branch_rows, l, 32), (w_out, l, 32)])
        x1, _ = _merge(l, ya, yb, yc, p, wb.reshape(N_BRANCH, BRANCH_W, D_MODEL), wo, xs, ada, ln1_g, ln1_b, last, [])
        xs, next_w = _ffn(l, x1, ada, wf_in, wf_out, ln2_g, ln2_b, last, [] if last else [(w_in_t, l + 1, 132)])
        if not last:
            w_t = next_w[0]
    return xs.reshape(BATCH, SEQ, D_MODEL)
```

```python
import functools
import math

import jax
import jax.numpy as jnp
from jax import lax
from jax.experimental import pallas as pl
from jax.experimental.pallas import tpu as pltpu

F32 = jnp.float32
BF16 = jnp.bfloat16

D_MODEL = 2048
BATCH = 4
SEQ = 2048
DEPTH = 2
GRID_W = 64
CTX_LEN = 256
LRU_WIDTH = 1024
LRU_BLOCKS = 8
LRU_BLOCK_W = LRU_WIDTH // LRU_BLOCKS
LRU_C = 8.0
CONV_W = 4
MLA_HEADS = 8
MLA_Q_RANK = 512
MLA_KV_RANK = 256
MLA_NOPE = 128
MLA_ROPE = 64
MLA_V = 128
MLA_SCALE = (MLA_NOPE + MLA_ROPE) ** -0.5
SWA_HEADS = 16
SWA_KV_HEADS = 4
SWA_HEAD_DIM = 64
SWA_REP = SWA_HEADS // SWA_KV_HEADS
SWA_SCALE = SWA_HEAD_DIM ** -0.5
WINDOW = 128
N_BRANCH = 3
BRANCH_W = 1024
MIX_IN = 4416
FFN_HIDDEN = -(-8 * D_MODEL // (3 * 256)) * 256
ROPE_BASE = 10000.0
LN_EPS = 1e-5
RMS_EPS = 1e-6
ALPHA = (2 * DEPTH) ** 0.25

T_TOK = CTX_LEN + SEQ
ROWS = BATCH * T_TOK
LANES = 128
SUBLANES = 8
MLA_SLOT = 2 * LANES

P_AX, P_AG, P_BQ, P_BKV, P_BKR = 0, 1024, 2048, 2560, 2816
P_CQ, P_CK, P_CV, P_GATE = 3072, 4096, 4352, 4608
P_COLS = P_GATE + N_BRANCH * D_MODEL

VMEM_LIMIT = 56 * 1024 * 1024

TM_MIX = 1152
TN_MIX = 1536
TM_MERGE = 576
TJ_MERGE = 512
TM_FFN = 768
TJ_FFN = 512
TM_PREP = 576
TQ_ATT = 256
TQ_MLA = 384
TN_ADA = 2048
TM_MERGE_LATENT = 512
TM_FFN_LATENT = 512
LOG2E = math.log2(math.e)


def _params(sem):
    return pltpu.CompilerParams(dimension_semantics=sem, vmem_limit_bytes=VMEM_LIMIT)


def _row_select(i, tiles_per_batch, tm, ctx_vec, batch_vec):
    rows = lax.broadcasted_iota(jnp.int32, (tm, 1), 0)
    n_ctx = jnp.where(i % tiles_per_batch == 0, CTX_LEN, 0)
    return jnp.where(rows < n_ctx, ctx_vec, batch_vec)


def _layer_norm(v, g, b):
    mu = jnp.mean(v, -1, keepdims=True)
    d = v - mu
    var = jnp.mean(d * d, -1, keepdims=True)
    return d * lax.rsqrt(var + LN_EPS) * g + b


def _rms_norm(v, g):
    return v * lax.rsqrt(jnp.mean(v * v, -1, keepdims=True) + RMS_EPS) * g


def _swap_pairs(x):
    lane = lax.broadcasted_iota(jnp.int32, (1, LANES), 1)
    first = (lane & 16) == 0
    return jnp.where(first, pltpu.roll(x, LANES - 16, 1), pltpu.roll(x, 16, 1))


def _rope(x, cos, sin):
    return x * cos + _swap_pairs(x) * sin


def _ada_kernel(c_ref, w_ref, b_ref, o_ref, *, chunks_per_vec):
    j = pl.program_id(1)
    c = c_ref[...]
    a = (c * jax.nn.sigmoid(c)).astype(BF16)
    acc = jnp.dot(a, w_ref[0].astype(BF16), preferred_element_type=F32)
    vec = j // chunks_per_vec
    one = jnp.where((vec == 1) | (vec == 4), 1.0, 0.0)
    o_ref[0, 0] = acc + b_ref[0] + one


def _ada(cvec, w_ada, b_ada):
    tn = TN_ADA
    cpv = D_MODEL // tn
    return pl.pallas_call(
        functools.partial(_ada_kernel, chunks_per_vec=cpv),
        grid=(DEPTH, 6 * cpv),
        in_specs=[
            pl.BlockSpec((SUBLANES, D_MODEL), lambda l, j: (0, 0)),
            pl.BlockSpec((1, D_MODEL, tn), lambda l, j: (l, 0, j)),
            pl.BlockSpec((1, 1, tn), lambda l, j: (l, 0, j)),
        ],
        out_specs=pl.BlockSpec((1, 1, SUBLANES, tn), lambda l, j: (l, j // cpv, 0, j % cpv)),
        out_shape=jax.ShapeDtypeStruct((DEPTH, 6, SUBLANES, D_MODEL), F32),
        compiler_params=_params(("arbitrary", "arbitrary")),
        name="ada",
    )(cvec, w_ada, b_ada.reshape(DEPTH, 1, 6 * D_MODEL))


def _vec_specs(l, vec, tiles_per_batch):
    return [
        pl.BlockSpec((None, 1, 1, 1, D_MODEL), lambda i, j: (l, vec, i // tiles_per_batch, 0, 0)),
        pl.BlockSpec((None, 1, 1, 1, D_MODEL), lambda i, j: (l, vec, BATCH, 0, 0)),
    ]


def _layer_vec(l):
    return pl.BlockSpec((None, 1, D_MODEL), lambda i, j: (l, 0, 0))


def _row_spec(tm, width, col_block, latent_only):
    if not latent_only:
        return pl.BlockSpec((tm, width), lambda i, j: (i, col_block(j)))
    tiles = SEQ // tm

    def index(i, j):
        row = (i // tiles) * T_TOK + CTX_LEN + (i % tiles) * tm
        return pl.multiple_of(row, CTX_LEN), pl.multiple_of(col_block(j) * width, LANES)

    return pl.BlockSpec((pl.Element(tm), pl.Element(width)), index)


def _call_with_casts(body, *, grid, in_specs, out_spec, out_shape, args, casts, name, scratch_shapes=()):
    n_in, n_side = len(in_specs), len(casts)
    n_steps = math.prod(grid)
    side_in, side_out, side_shape = [], [], []
    for src, layer, n_blocks in casts:
        _, n_rows, n_cols = src.shape
        rows = n_rows // n_blocks
        assert rows * n_blocks == n_rows and rows % (2 * SUBLANES) == 0 and n_blocks <= n_steps

        def block(*ids, n_blocks=n_blocks):
            step = ids[0] if len(ids) == 1 else ids[0] * grid[1] + ids[1]
            return jnp.minimum(step, n_blocks - 1)

        side_in.append(pl.BlockSpec((None, rows, n_cols), lambda *ids, block=block, layer=layer: (layer, block(*ids), 0)))
        side_out.append(pl.BlockSpec((rows, n_cols), lambda *ids, block=block: (block(*ids), 0)))
        side_shape.append(jax.ShapeDtypeStruct((n_rows, n_cols), BF16))

    def kernel(*refs):
        for src_ref, dst_ref in zip(refs[n_in:n_in + n_side], refs[n_in + n_side + 1:n_in + 2 * n_side + 1]):
            dst_ref[...] = src_ref[...].astype(dst_ref.dtype)
        body(*refs[:n_in], refs[n_in + n_side], *refs[n_in + 2 * n_side + 1:])

    out = pl.pallas_call(
        kernel,
        grid=grid,
        in_specs=list(in_specs) + side_in,
        out_specs=[out_spec] + side_out,
        out_shape=[out_shape] + side_shape,
        scratch_shapes=list(scratch_shapes),
        compiler_params=_params(("arbitrary",) * len(grid)),
        name=name,
    )(*args, *[c[0] for c in casts])
    return out[0], list(out[1:])


def _mixin_kernel(x_ref, scb_ref, scc_ref, shb_ref, shc_ref, w_ref, o_ref, u_ref, *, tpb, tm):
    i = pl.program_id(0)

    @pl.when(pl.program_id(1) == 0)
    def _():
        sc = _row_select(i, tpb, tm, scc_ref[0, 0], scb_ref[0, 0])
        sh = _row_select(i, tpb, tm, shc_ref[0, 0], shb_ref[0, 0])
        u_ref[...] = (x_ref[...] * sc + sh).astype(BF16)

    o_ref[...] = lax.dot_general(u_ref[...], w_ref[...], (((1,), (1,)), ((), ())),
                                 preferred_element_type=F32).astype(o_ref.dtype)


def _mixin(l, x, ada, w_t, casts):
    tm, tn = TM_MIX, TN_MIX
    tpb = T_TOK // tm
    pad = P_CQ - (P_BKR + MLA_ROPE)
    assert P_CQ % tn == 0 and w_t.shape[0] + pad == P_COLS

    def w_index(i, j):
        row = j * tn - jnp.where(j >= P_CQ // tn, pad, 0)
        return pl.multiple_of(row, MLA_ROPE), 0

    return _call_with_casts(
        functools.partial(_mixin_kernel, tpb=tpb, tm=tm),
        grid=(ROWS // tm, P_COLS // tn),
        in_specs=[pl.BlockSpec((tm, D_MODEL), lambda i, j: (i, 0))]
        + _vec_specs(l, 1, tpb)
        + _vec_specs(l, 0, tpb)
        + [pl.BlockSpec((pl.Element(tn), pl.Element(D_MODEL)), w_index)],
        out_spec=pl.BlockSpec((tm, tn), lambda i, j: (i, j)),
        out_shape=jax.ShapeDtypeStruct((ROWS, P_COLS), BF16),
        scratch_shapes=[pltpu.VMEM((tm, D_MODEL), BF16)],
        args=(x, ada, ada, ada, ada, w_t),
        casts=casts,
        name="mix_in",
    )


LRU_NB = 2
LRU_SEG_FWD = 292
LRU_SEG_BWD = 260
LRU_PAD_TOK = SUBLANES * LRU_SEG_FWD
assert LRU_PAD_TOK >= T_TOK and CTX_LEN + SUBLANES * LRU_SEG_BWD == LRU_PAD_TOK
assert LRU_SEG_FWD % 8 == 4 and LRU_SEG_BWD % 8 == 4


def _lru_kernel(ax_ref, ag_ref, cw_ref, cb_ref, wr_ref, br_ref, wi_ref, bi_ref, lam_ref, y_ref,
                a_s, b_s, h_s, p_s):
    n_tok = T_TOK
    nb = LRU_NB
    t = lax.broadcasted_iota(jnp.int32, (n_tok, 1), 0)
    is_ctx = t < CTX_LEN
    t_loc = jnp.where(is_ctx, t, t - CTX_LEN)
    seg_len = jnp.where(is_ctx, CTX_LEN, SEQ)

    for k in range(nb):
        lanes = slice(LANES * k, LANES * (k + 1))
        x = ax_ref[:, lanes].astype(F32)
        xm1 = jnp.where(t_loc >= 1, pltpu.roll(x, 1, 0), 0.0)
        xm2 = jnp.where(t_loc >= 2, pltpu.roll(x, 2, 0), 0.0)
        xp1 = jnp.where(t_loc <= seg_len - 2, pltpu.roll(x, n_tok - 1, 0), 0.0)
        w = cw_ref[:, lanes]
        xc = xm2 * w[0:1] + xm1 * w[1:2] + x * w[2:3] + xp1 * w[3:4] + cb_ref[:, lanes]
        xb = xc.astype(BF16)
        for d in range(2):
            i = d * nb + k
            tr = jnp.tanh(0.5 * (jnp.dot(xb, wr_ref[d, k], preferred_element_type=F32) + br_ref[d, :, lanes]))
            ti = jnp.tanh(0.5 * (jnp.dot(xb, wi_ref[d, k], preferred_element_type=F32) + bi_ref[d, :, lanes]))
            z = -lam_ref[d, :, lanes]
            softplus = jnp.maximum(z, 0.0) + jnp.log1p(jnp.exp(-jnp.abs(z)))
            half_c = (-0.5 * LRU_C) * softplus
            log_a = half_c * tr + half_c
            a = jnp.exp(log_a)
            v = jnp.tanh(-log_a) * (a * a + 1.0)
            mult = jnp.where(v > 0.0, v * lax.rsqrt(v), 0.0)
            a_s[i, 0:n_tok, :] = a
            b_s[i, 0:n_tok, :] = (0.5 * mult) * (ti + 1.0) * xc
            a_s[i, n_tok:LRU_PAD_TOK, :] = jnp.ones((LRU_PAD_TOK - n_tok, LANES), F32)
            b_s[i, n_tok:LRU_PAD_TOK, :] = jnp.zeros((LRU_PAD_TOK - n_tok, LANES), F32)

    sub = lax.broadcasted_iota(jnp.int32, (CTX_LEN, 1), 0) & (SUBLANES - 1)
    for k in range(nb):
        a = a_s[nb + k, 0:CTX_LEN, :]
        bb = b_s[nb + k, 0:CTX_LEN, :]
        for s in (1, 2, 4):
            m = sub < SUBLANES - s
            a_sh = pltpu.roll(a, CTX_LEN - s, 0)
            b_sh = pltpu.roll(bb, CTX_LEN - s, 0)
            bb = jnp.where(m, bb + a * b_sh, bb)
            a = jnp.where(m, a * a_sh, a)
        a_s[nb + k, 0:CTX_LEN, :] = a
        b_s[nb + k, 0:CTX_LEN, :] = bb

    def ctx_bwd(j, hs):
        rows = pl.ds(pl.multiple_of((CTX_LEN // SUBLANES - 1 - j) * SUBLANES, SUBLANES), SUBLANES)
        out = []
        for k in range(nb):
            hg = b_s[nb + k, rows, :] + a_s[nb + k, rows, :] * hs[k]
            b_s[nb + k, rows, :] = hg
            out.append(hg[0:1])
        return tuple(out)

    h_ctx0 = lax.fori_loop(0, CTX_LEN // SUBLANES, ctx_bwd, (jnp.zeros((1, LANES), F32),) * nb, unroll=4)

    def seg_rows(base, s, seg):
        return pl.ds(base + s, SUBLANES, stride=seg)

    def scan_step(i, rows, h, p):
        av = a_s[i, rows, :]
        h = av * h + b_s[i, rows, :]
        p = av * p
        h_s[i, rows, :] = h
        p_s[i, rows, :] = p
        return h, p

    def fwd_steps(s, c):
        rows = seg_rows(0, s, LRU_SEG_FWD)
        return sum((scan_step(k, rows, c[2 * k], c[2 * k + 1]) for k in range(nb)), ())

    def bwd_steps(s, c):
        rows = seg_rows(CTX_LEN, LRU_SEG_BWD - 1 - s, LRU_SEG_BWD)
        return sum((scan_step(nb + k, rows, c[2 * k], c[2 * k + 1]) for k in range(nb)), ())

    zeros8 = jnp.zeros((SUBLANES, LANES), F32)
    ones8 = jnp.ones((SUBLANES, LANES), F32)
    init = (zeros8, ones8) * nb

    def both(s, c):
        return fwd_steps(s, c[:2 * nb]) + bwd_steps(s, c[2 * nb:])

    c = lax.fori_loop(0, LRU_SEG_BWD, both, init + init, unroll=2)
    cf_state = lax.fori_loop(LRU_SEG_BWD, LRU_SEG_FWD, fwd_steps, c[:2 * nb], unroll=2)
    cb_state = c[2 * nb:]

    seg = lax.broadcasted_iota(jnp.int32, (SUBLANES, 1), 0)
    cfs, cbs = [], []
    for k in range(nb):
        hf, pf = cf_state[2 * k], cf_state[2 * k + 1]
        hb, pb = cb_state[2 * k], cb_state[2 * k + 1]
        cf = zeros8
        cb = jnp.where(seg == SUBLANES - 1, h_ctx0[k], 0.0)
        for _ in range(SUBLANES - 1):
            cf = jnp.where(seg >= 1, pltpu.roll(hf + pf * cf, 1, 0), 0.0)
            cb = jnp.where(seg <= SUBLANES - 2, pltpu.roll(hb + pb * cb, SUBLANES - 1, 0), h_ctx0[k])
        cfs.append(cf)
        cbs.append(cb)

    def fix_fwd(s):
        rows = seg_rows(0, s, LRU_SEG_FWD)
        for k in range(nb):
            b_s[k, rows, :] = h_s[k, rows, :] + p_s[k, rows, :] * cfs[k]

    def fix_bwd(s):
        rows = seg_rows(CTX_LEN, s, LRU_SEG_BWD)
        for k in range(nb):
            b_s[nb + k, rows, :] = h_s[nb + k, rows, :] + p_s[nb + k, rows, :] * cbs[k]

    def fix_both(s, c):
        fix_fwd(s)
        fix_bwd(s)
        return c

    def fix_tail(s, c):
        fix_fwd(s)
        return c

    lax.fori_loop(0, LRU_SEG_BWD, fix_both, 0, unroll=2)
    lax.fori_loop(LRU_SEG_BWD, LRU_SEG_FWD, fix_tail, 0, unroll=2)

    for k in range(nb):
        lanes = slice(LANES * k, LANES * (k + 1))
        g = ag_ref[:, lanes].astype(F32)
        cdf = 0.5 * (1.0 + jnp.tanh(math.sqrt(2.0 / math.pi) * (g + 0.044715 * (g * g * g))))
        h_sum = b_s[k, 0:n_tok, :] + b_s[nb + k, 0:n_tok, :]
        y_ref[:, lanes] = (h_sum * (g * cdf)).astype(y_ref.dtype)


def _lru(l, p, conv_w, conv_b, wr, br, wi, bi, lam, casts):
    assert LRU_BLOCK_W == LANES
    cw = LRU_NB * LANES
    col = lambda off: (lambda b, n: (b, off // cw + n))
    vec2 = pl.BlockSpec((None, 2, 1, cw), lambda b, n: (l, 0, 0, n))
    mat = pl.BlockSpec((None, 2, LRU_NB, LRU_BLOCK_W, LRU_BLOCK_W), lambda b, n: (l, 0, n, 0, 0))
    scan_buf = pltpu.VMEM((2 * LRU_NB, LRU_PAD_TOK, LANES), F32)
    return _call_with_casts(
        _lru_kernel,
        grid=(BATCH, LRU_WIDTH // cw),
        in_specs=[
            pl.BlockSpec((T_TOK, cw), col(P_AX)),
            pl.BlockSpec((T_TOK, cw), col(P_AG)),
            pl.BlockSpec((None, CONV_W, cw), lambda b, n: (l, 0, n)),
            pl.BlockSpec((None, 1, cw), lambda b, n: (l, 0, n)),
            mat, vec2, mat, vec2, vec2,
        ],
        out_spec=pl.BlockSpec((T_TOK, cw), lambda b, n: (b, n)),
        out_shape=jax.ShapeDtypeStruct((ROWS, LRU_WIDTH), BF16),
        scratch_shapes=[scan_buf, scan_buf, scan_buf, scan_buf],
        args=(p, p, conv_w, conv_b.reshape(DEPTH, 1, LRU_WIDTH), wr, br.reshape(DEPTH, 2, 1, LRU_WIDTH), wi,
              bi.reshape(DEPTH, 2, 1, LRU_WIDTH), lam.reshape(DEPTH, 2, 1, LRU_WIDTH)),
        casts=casts,
        name="rg_lru",
    )


def _mla_prep_kernel(bq_ref, bkv_ref, bkr_ref, qn_ref, wq_ref, kvn_ref, wk_ref, wv_ref, cos_ref, sin_ref,
                     q_out, k_out, v_out):
    cos = cos_ref[...]
    sin = sin_ref[...]
    qn = _rms_norm(bq_ref[...].astype(F32), qn_ref[...])
    q = jnp.dot(qn.astype(BF16), wq_ref[...], preferred_element_type=F32) * (MLA_SCALE * LOG2E)
    kvn = _rms_norm(bkv_ref[...].astype(F32), kvn_ref[...]).astype(BF16)
    kn = jnp.dot(kvn, wk_ref[...], preferred_element_type=F32).astype(BF16)
    v_out[...] = jnp.dot(kvn, wv_ref[...], preferred_element_type=F32).astype(BF16)
    lane = lax.broadcasted_iota(jnp.int32, (1, LANES), 1)
    kr = jnp.where(lane < MLA_ROPE, bkr_ref[...].astype(F32), 0.0)
    kr = _rope(kr, cos, sin).astype(BF16)
    for h in range(MLA_HEADS):
        lo = MLA_SLOT * h
        q_out[:, lo:lo + LANES] = q[:, lo:lo + LANES].astype(BF16)
        q_out[:, lo + LANES:lo + MLA_SLOT] = _rope(q[:, lo + LANES:lo + MLA_SLOT], cos, sin).astype(BF16)
        k_out[:, lo:lo + LANES] = kn[:, LANES * h:LANES * (h + 1)]
        k_out[:, lo + LANES:lo + MLA_SLOT] = kr


def _mla_prep(l, p, q_norm, wq_p, kv_norm, wk_p, wv_p, cos, sin):
    tm = TM_PREP
    tpb = T_TOK // tm
    full = lambda shape: pl.BlockSpec((None,) + shape, lambda i: (l,) + (0,) * len(shape))
    return pl.pallas_call(
        _mla_prep_kernel,
        grid=(ROWS // tm,),
        in_specs=[
            pl.BlockSpec((tm, MLA_Q_RANK), lambda i: (i, P_BQ // MLA_Q_RANK)),
            pl.BlockSpec((tm, MLA_KV_RANK), lambda i: (i, P_BKV // MLA_KV_RANK)),
            pl.BlockSpec((tm, LANES), lambda i: (i, P_BKR // LANES)),
            full((1, MLA_Q_RANK)),
            full((MLA_Q_RANK, MLA_HEADS * MLA_SLOT)),
            full((1, MLA_KV_RANK)),
            full((MLA_KV_RANK, MLA_HEADS * MLA_NOPE)),
            full((MLA_KV_RANK, MLA_HEADS * MLA_V)),
            pl.BlockSpec((tm, LANES), lambda i: (i % tpb, 0)),
            pl.BlockSpec((tm, LANES), lambda i: (i % tpb, 0)),
        ],
        out_specs=[
            pl.BlockSpec((tm, MLA_HEADS * MLA_SLOT), lambda i: (i, 0)),
            pl.BlockSpec((tm, MLA_HEADS * MLA_SLOT), lambda i: (i, 0)),
            pl.BlockSpec((tm, MLA_HEADS * MLA_V), lambda i: (i, 0)),
        ],
        out_shape=[
            jax.ShapeDtypeStruct((ROWS, MLA_HEADS * MLA_SLOT), BF16),
            jax.ShapeDtypeStruct((ROWS, MLA_HEADS * MLA_SLOT), BF16),
            jax.ShapeDtypeStruct((ROWS, MLA_HEADS * MLA_V), BF16),
        ],
        compiler_params=_params(("arbitrary",)),
        name="mla_prep",
    )(p, p, p, q_norm.reshape(DEPTH, 1, -1), wq_p, kv_norm.reshape(DEPTH, 1, -1), wk_p, wv_p, cos, sin)


def _mla_attn_kernel(q_ref, k_ref, v_ref, o_ref):
    tq = q_ref.shape[0]

    def attend(row0, n_rows, n_keys):
        for h in range(MLA_HEADS):
            q = q_ref[row0:row0 + n_rows, MLA_SLOT * h:MLA_SLOT * (h + 1)]
            k = k_ref[0:n_keys, MLA_SLOT * h:MLA_SLOT * (h + 1)]
            s = lax.dot_general(q, k, (((1,), (1,)), ((), ())), preferred_element_type=F32)
            m = jnp.max(s, -1, keepdims=True)
            p = jnp.exp2(s - m)
            l = jnp.sum(p, -1, keepdims=True)
            o = jnp.dot(p.astype(BF16), v_ref[0:n_keys, MLA_V * h:MLA_V * (h + 1)], preferred_element_type=F32)
            o_ref[row0:row0 + n_rows, MLA_V * h:MLA_V * (h + 1)] = (o / l).astype(o_ref.dtype)

    @pl.when(pl.program_id(1) == 0)
    def _():
        attend(0, CTX_LEN, CTX_LEN)
        attend(CTX_LEN, tq - CTX_LEN, T_TOK)

    @pl.when(pl.program_id(1) > 0)
    def _():
        attend(0, tq, T_TOK)


def _mla_attn(q, k, v):
    tq = TQ_MLA
    assert tq > CTX_LEN and T_TOK % tq == 0 and tq % (2 * SUBLANES) == 0
    tpb = T_TOK // tq
    per_batch = lambda width: pl.BlockSpec((T_TOK, width), lambda b, i: (b, 0))
    return pl.pallas_call(
        _mla_attn_kernel,
        grid=(BATCH, tpb),
        in_specs=[
            pl.BlockSpec((tq, MLA_HEADS * MLA_SLOT), lambda b, i: (b * tpb + i, 0)),
            per_batch(MLA_HEADS * MLA_SLOT),
            per_batch(MLA_HEADS * MLA_V),
        ],
        out_specs=pl.BlockSpec((tq, MLA_HEADS * MLA_V), lambda b, i: (b * tpb + i, 0)),
        out_shape=jax.ShapeDtypeStruct((ROWS, MLA_HEADS * MLA_V), BF16),
        compiler_params=_params(("arbitrary", "arbitrary")),
        name="mla_attn",
    )(q, k, v)


def _dup_half(x, upper):
    lane = lax.broadcasted_iota(jnp.int32, (1, LANES), 1)
    r = pltpu.roll(x, LANES // 2, 1)
    out = jnp.where(lane < LANES // 2, r, x) if upper else jnp.where(lane < LANES // 2, x, r)
    return out.astype(BF16)


def _swa_kernel(q_ref, k_ref, v_ref, cos_ref, sin_ref, sink_ref, o_ref, *, layer):
    tq = TQ_ATT
    band = tq + 2 * WINDOW
    qt = pl.program_id(1)
    q_row = pl.multiple_of(qt * tq, tq)
    lane = lax.broadcasted_iota(jnp.int32, (1, LANES), 1)
    low = lane < LANES // 2
    cos_q = cos_ref[pl.ds(q_row, tq), :]
    sin_q = sin_ref[pl.ds(q_row, tq), :]

    def q_block(c):
        x = q_ref[:, LANES * c:LANES * (c + 1)].astype(F32)
        return _rope(x, cos_q, sin_q) * (SWA_SCALE * LOG2E)

    def softmax_out(qm, sink, keys, vals, masks):
        ss = []
        for kk, mk in zip(keys, masks):
            s = lax.dot_general(qm, kk, (((1,), (1,)), ((), ())), preferred_element_type=F32)
            ss.append(s if mk is None else jnp.where(mk, s, -jnp.inf))
        m = jnp.maximum(functools.reduce(jnp.maximum, [jnp.max(s, -1, keepdims=True) for s in ss]), sink)
        ps = [jnp.exp2(s - m) for s in ss]
        l = functools.reduce(jnp.add, [jnp.sum(p, -1, keepdims=True) for p in ps]) + jnp.exp2(sink - m)
        o = functools.reduce(jnp.add, [jnp.dot(p.astype(BF16), vv, preferred_element_type=F32)
                                       for p, vv in zip(ps, vals)])
        return o / l

    def run(keys_of, vals_of, masks):
        for g in range(SWA_KV_HEADS):
            keys = keys_of(g)
            vals = vals_of(g)
            qms, sinks = [], []
            for c in (2 * g, 2 * g + 1):
                qc = q_block(c)
                qms += [jnp.where(low, qc, 0.0).astype(BF16), jnp.where(low, 0.0, qc).astype(BF16)]
                sinks += [sink_ref[layer, 2 * c] * LOG2E, sink_ref[layer, 2 * c + 1] * LOG2E]
            scores = [[lax.dot_general(qm, kk, (((1,), (1,)), ((), ())), preferred_element_type=F32) for qm in qms]
                      for kk in keys]
            ps, ls = [], []
            for h in range(SWA_REP):
                ss = [s[h] if mk is None else jnp.where(mk, s[h], -jnp.inf) for s, mk in zip(scores, masks)]
                m = jnp.maximum(functools.reduce(jnp.maximum, [jnp.max(s, -1, keepdims=True) for s in ss]), sinks[h])
                pr = [jnp.exp2(s - m) for s in ss]
                ls.append(functools.reduce(jnp.add, [jnp.sum(p, -1, keepdims=True) for p in pr]) + jnp.exp2(sinks[h] - m))
                ps.append([p.astype(BF16) for p in pr])
            outs = [[jnp.dot(ps[h][b], vv, preferred_element_type=F32) for h in range(SWA_REP)]
                    for b, vv in enumerate(vals)]
            os_ = [functools.reduce(jnp.add, [outs[b][h] for b in range(len(vals))]) / ls[h] for h in range(SWA_REP)]
            for i, c in enumerate((2 * g, 2 * g + 1)):
                o_ref[:, LANES * c:LANES * (c + 1)] = jnp.where(low, os_[2 * i], os_[2 * i + 1]).astype(o_ref.dtype)

    def ctx_kv(ref, g):
        return _dup_half(ref[0:CTX_LEN, LANES * (g // 2):LANES * (g // 2 + 1)].astype(F32), g % 2 == 1)

    @pl.when(qt == 0)
    def _():
        if layer == DEPTH - 1:
            o_ref[...] = jnp.zeros_like(o_ref)
        else:
            run(lambda g: [ctx_kv(k_ref, g)], lambda g: [ctx_kv(v_ref, g)], [None])

    @pl.when(qt > 0)
    def _():
        start = pl.multiple_of(jnp.minimum(q_row - WINDOW, T_TOK - band), LANES)
        qpos = q_row - CTX_LEN + lax.broadcasted_iota(jnp.int32, (tq, 1), 0)
        kpos = start - CTX_LEN + lax.broadcasted_iota(jnp.int32, (1, band), 1)
        valid = (jnp.abs(kpos - qpos) <= WINDOW) & (kpos >= 0)
        cos_k = cos_ref[pl.ds(start, band), :]
        sin_k = sin_ref[pl.ds(start, band), :]

        def band_k(g):
            kb = k_ref[pl.ds(start, band), LANES * (g // 2):LANES * (g // 2 + 1)].astype(F32)
            return _dup_half(_rope(kb, cos_k, sin_k), g % 2 == 1)

        def band_v(g):
            return _dup_half(v_ref[pl.ds(start, band), LANES * (g // 2):LANES * (g // 2 + 1)].astype(F32), g % 2 == 1)

        run(lambda g: [ctx_kv(k_ref, g), band_k(g)], lambda g: [ctx_kv(v_ref, g), band_v(g)], [None, valid])


def _swa(l, p, cos, sin, sinks, casts):
    tq = TQ_ATT
    tpb = T_TOK // tq
    kvw = SWA_KV_HEADS * SWA_HEAD_DIM
    return _call_with_casts(
        functools.partial(_swa_kernel, layer=l),
        grid=(BATCH, tpb),
        in_specs=[
            pl.BlockSpec((tq, SWA_HEADS * SWA_HEAD_DIM), lambda b, i: (b * tpb + i, P_CQ // (SWA_HEADS * SWA_HEAD_DIM))),
            pl.BlockSpec((T_TOK, kvw), lambda b, i: (b, P_CK // kvw)),
            pl.BlockSpec((T_TOK, kvw), lambda b, i: (b, P_CV // kvw)),
            pl.BlockSpec((T_TOK, LANES), lambda b, i: (0, 0)),
            pl.BlockSpec((T_TOK, LANES), lambda b, i: (0, 0)),
            pl.BlockSpec(memory_space=pltpu.SMEM),
        ],
        out_spec=pl.BlockSpec((tq, SWA_HEADS * SWA_HEAD_DIM), lambda b, i: (b * tpb + i, 0)),
        out_shape=jax.ShapeDtypeStruct((ROWS, SWA_HEADS * SWA_HEAD_DIM), BF16),
        args=(p, p, p, cos, sin, sinks),
        casts=casts,
        name="swa",
    )


def _gate_rows(i, tpb, tm, gvc_ref, gvb_ref, has_ctx):
    return _row_select(i, tpb, tm, gvc_ref[0, 0], gvb_ref[0, 0]) if has_ctx else gvb_ref[0, 0]


def _merge_kernel(ya_ref, yb_ref, yc_ref, ga_ref, gb_ref, gc_ref, wb_ref, wo_ref, x_ref, gvb_ref, gvc_ref,
                  lng_ref, lnb_ref, o_ref, *, tpb, tm, nj, has_ctx):
    i = pl.program_id(0)
    j = pl.program_id(1)

    @pl.when(j == 0)
    def _():
        o_ref[...] = jnp.zeros_like(o_ref)

    z = None
    for n, (y_ref, g_ref) in enumerate(((ya_ref, ga_ref), (yb_ref, gb_ref), (yc_ref, gc_ref))):
        zn = jnp.dot(y_ref[...], wb_ref[n], preferred_element_type=F32)
        zn = jax.nn.sigmoid(g_ref[...].astype(F32)) * zn
        z = zn if z is None else z + zn
    o_ref[...] += jnp.dot(z.astype(BF16), wo_ref[...], preferred_element_type=F32)

    @pl.when(j == nj - 1)
    def _():
        gate = _gate_rows(i, tpb, tm, gvc_ref, gvb_ref, has_ctx)
        o_ref[...] = _layer_norm(ALPHA * x_ref[...] + gate * o_ref[...], lng_ref[...], lnb_ref[...])


def _merge(l, ya, yb, yc, p, wb, wo, x, ada, ln_g, ln_b, latent_only, casts):
    tm = TM_MERGE_LATENT if latent_only else TM_MERGE
    tj = TJ_MERGE
    tpb = (SEQ if latent_only else T_TOK) // tm
    n_rows = BATCH * SEQ if latent_only else ROWS
    nj = D_MODEL // tj
    ybs = _row_spec(tm, BRANCH_W, lambda j: 0, latent_only)
    gate = lambda n: _row_spec(tm, tj, lambda j: (P_GATE + n * D_MODEL) // tj + j, latent_only)
    return _call_with_casts(
        functools.partial(_merge_kernel, tpb=tpb, tm=tm, nj=nj, has_ctx=not latent_only),
        grid=(n_rows // tm, nj),
        in_specs=[ybs, ybs, ybs, gate(0), gate(1), gate(2),
                  pl.BlockSpec((N_BRANCH, BRANCH_W, tj), lambda i, j: (0, 0, j)),
                  pl.BlockSpec((tj, D_MODEL), lambda i, j: (j, 0)),
                  _row_spec(tm, D_MODEL, lambda j: 0, latent_only)]
        + _vec_specs(l, 2, tpb) + [_layer_vec(l), _layer_vec(l)],
        out_spec=pl.BlockSpec((tm, D_MODEL), lambda i, j: (i, 0)),
        out_shape=jax.ShapeDtypeStruct((n_rows, D_MODEL), F32),
        args=(ya, yb, yc, p, p, p, wb, wo, x, ada, ada, ln_g.reshape(DEPTH, 1, -1), ln_b.reshape(DEPTH, 1, -1)),
        casts=casts,
        name="merge",
    )


def _ffn_kernel(x_ref, scb_ref, scc_ref, shb_ref, shc_ref, gvb_ref, gvc_ref, wg_ref, wu_ref, wo_ref,
                lng_ref, lnb_ref, o_ref, u_ref, *, tpb, tm, nj, has_ctx):
    i = pl.program_id(0)
    j = pl.program_id(1)

    @pl.when(j == 0)
    def _():
        sc = _gate_rows(i, tpb, tm, scc_ref, scb_ref, has_ctx)
        sh = _gate_rows(i, tpb, tm, shc_ref, shb_ref, has_ctx)
        u_ref[...] = (x_ref[...] * sc + sh).astype(BF16)
        o_ref[...] = jnp.zeros_like(o_ref)

    u = u_ref[...]
    gt = jnp.dot(u, wg_ref[...], preferred_element_type=F32)
    up = jnp.dot(u, wu_ref[...], preferred_element_type=F32)
    h = (gt * jax.nn.sigmoid(gt) * up).astype(BF16)
    o_ref[...] += jnp.dot(h, wo_ref[...], preferred_element_type=F32)

    @pl.when(j == nj - 1)
    def _():
        gate = _gate_rows(i, tpb, tm, gvc_ref, gvb_ref, has_ctx)
        o_ref[...] = _layer_norm(ALPHA * x_ref[...] + gate * o_ref[...], lng_ref[...], lnb_ref[...])


def _ffn(l, x, ada, w_in, w_out, ln_g, ln_b, latent_only, casts):
    tm = TM_FFN_LATENT if latent_only else TM_FFN
    tj = TJ_FFN
    tpb = (SEQ if latent_only else T_TOK) // tm
    n_rows = x.shape[0]
    nj = FFN_HIDDEN // tj
    row = pl.BlockSpec((tm, D_MODEL), lambda i, j: (i, 0))
    row_in = row
    return _call_with_casts(
        functools.partial(_ffn_kernel, tpb=tpb, tm=tm, nj=nj, has_ctx=not latent_only),
        grid=(n_rows // tm, nj),
        in_specs=[row_in] + _vec_specs(l, 4, tpb) + _vec_specs(l, 3, tpb) + _vec_specs(l, 5, tpb) + [
            pl.BlockSpec((D_MODEL, tj), lambda i, j: (0, j)),
            pl.BlockSpec((D_MODEL, tj), lambda i, j: (0, nj + j)),
            pl.BlockSpec((tj, D_MODEL), lambda i, j: (j, 0)),
            _layer_vec(l), _layer_vec(l)],
        out_spec=row,
        out_shape=jax.ShapeDtypeStruct((n_rows, D_MODEL), F32),
        scratch_shapes=[pltpu.VMEM((tm, D_MODEL), BF16)],
        args=(x, ada, ada, ada, ada, ada, ada, w_in, w_in, w_out, ln_g.reshape(DEPTH, 1, -1), ln_b.reshape(DEPTH, 1, -1)),
        casts=casts,
        name="ffn",
    )


def _rope_tables(both_halves):
    rot = SWA_HEAD_DIM
    half = rot // 2
    n_rows = SEQ // GRID_W
    row = jnp.repeat(jnp.arange(n_rows, dtype=F32), GRID_W)
    colp = jnp.tile(jnp.arange(GRID_W, dtype=F32), n_rows)
    inv = ROPE_BASE ** (-jnp.arange(0, half, 2, dtype=F32) / half)
    ang_r = row[:, None] * inv
    ang_c = colp[:, None] * inv
    cos = jnp.concatenate([jnp.cos(ang_r), jnp.cos(ang_r), jnp.cos(ang_c), jnp.cos(ang_c)], -1)
    sin = jnp.concatenate([-jnp.sin(ang_r), jnp.sin(ang_r), -jnp.sin(ang_c), jnp.sin(ang_c)], -1)
    if both_halves:
        cos = jnp.concatenate([cos, cos], -1)
        sin = jnp.concatenate([sin, sin], -1)
    else:
        cos = jnp.concatenate([cos, jnp.ones_like(cos)], -1)
        sin = jnp.concatenate([sin, jnp.zeros_like(sin)], -1)
    cos = jnp.concatenate([jnp.ones((CTX_LEN, LANES), F32), cos], 0)
    sin = jnp.concatenate([jnp.zeros((CTX_LEN, LANES), F32), sin], 0)
    return cos, sin


def _mla_weights(w_q_up, w_kv_up):
    wq = w_q_up.reshape(DEPTH, MLA_Q_RANK, MLA_HEADS, MLA_NOPE + MLA_ROPE)
    wq = jnp.pad(wq, ((0, 0), (0, 0), (0, 0), (0, MLA_SLOT - MLA_NOPE - MLA_ROPE)))
    wkv = w_kv_up.reshape(DEPTH, MLA_KV_RANK, MLA_HEADS, MLA_NOPE + MLA_V)
    wk = wkv[..., :MLA_NOPE].reshape(DEPTH, MLA_KV_RANK, MLA_HEADS * MLA_NOPE)
    wv = wkv[..., MLA_NOPE:].reshape(DEPTH, MLA_KV_RANK, MLA_HEADS * MLA_V)
    return wq.reshape(DEPTH, MLA_Q_RANK, MLA_HEADS * MLA_SLOT).astype(BF16), wk.astype(BF16), wv.astype(BF16)


def kernel(x, c, ctx, c_ctx, w_ada, b_ada, w_in, conv_w, conv_b, lru_wr, lru_br, lru_wi, lru_bi, lru_lambda, mla_q_norm, mla_w_q_up, mla_kv_norm, mla_w_kv_up, swa_sinks, w_branch, w_out, ln1_g, ln1_b, w_ffn_in, w_ffn_out, ln2_g, ln2_b):
    cos_mla, sin_mla = _rope_tables(False)
    cos_swa, sin_swa = _rope_tables(True)
    cvec = jnp.concatenate([c, c_ctx[None], jnp.zeros((SUBLANES - BATCH - 1, D_MODEL), F32)], 0)
    ada = _ada(cvec, w_ada, b_ada).reshape(DEPTH, 6, SUBLANES, 1, D_MODEL)
    w_in_t = jnp.transpose(w_in, (0, 2, 1))
    w_t = w_in_t[0].astype(BF16)
    w_branch_rows = w_branch.reshape(DEPTH, N_BRANCH * BRANCH_W, D_MODEL)
    wr, wi = lru_wr.astype(BF16), lru_wi.astype(BF16)
    wq_p, wk_p, wv_p = _mla_weights(mla_w_q_up, mla_w_kv_up)
    xs = jnp.concatenate([ctx, x], 1).reshape(ROWS, D_MODEL)
    for l in range(DEPTH):
        last = l == DEPTH - 1
        p, (wf_out,) = _mixin(l, xs, ada, w_t, [(w_ffn_out, l, 44)])
        ya, (wf_in,) = _lru(l, p, conv_w, conv_b, wr, lru_br, wi, lru_bi, lru_lambda, [(w_ffn_in, l, 16)])
        q, k, v = _mla_prep(l, p, mla_q_norm, wq_p, mla_kv_norm, wk_p, wv_p, cos_mla, sin_mla)
        yb = _mla_attn(q, k, v)
        yc, (wb, wo) = _swa(l, p, cos_swa, sin_swa, swa_sinks, [(w_branch_rows, l, 32), (w_out, l, 32)])
        x1, _ = _merge(l, ya, yb, yc, p, wb.reshape(N_BRANCH, BRANCH_W, D_MODEL), wo, xs, ada, ln1_g, ln1_b, last, [])
        xs, next_w = _ffn(l, x1, ada, wf_in, wf_out, ln2_g, ln2_b, last, [] if last else [(w_in_t, l + 1, 132)])
        if not last:
            w_t = next_w[0]
    return xs.reshape(BATCH, SEQ, D_MODEL)
```

```python
import functools
import math

import jax
import jax.numpy as jnp
from jax import lax
from jax.experimental import pallas as pl
from jax.experimental.pallas import tpu as pltpu

F32 = jnp.float32
BF16 = jnp.bfloat16

D_MODEL = 2048
BATCH = 4
SEQ = 2048
DEPTH = 2
GRID_W = 64
CTX_LEN = 256
LRU_WIDTH = 1024
LRU_BLOCKS = 8
LRU_BLOCK_W = LRU_WIDTH // LRU_BLOCKS
LRU_C = 8.0
CONV_W = 4
MLA_HEADS = 8
MLA_Q_RANK = 512
MLA_KV_RANK = 256
MLA_NOPE = 128
MLA_ROPE = 64
MLA_V = 128
MLA_SCALE = (MLA_NOPE + MLA_ROPE) ** -0.5
SWA_HEADS = 16
SWA_KV_HEADS = 4
SWA_HEAD_DIM = 64
SWA_REP = SWA_HEADS // SWA_KV_HEADS
SWA_SCALE = SWA_HEAD_DIM ** -0.5
WINDOW = 128
N_BRANCH = 3
BRANCH_W = 1024
MIX_IN = 4416
FFN_HIDDEN = -(-8 * D_MODEL // (3 * 256)) * 256
ROPE_BASE = 10000.0
LN_EPS = 1e-5
RMS_EPS = 1e-6
ALPHA = (2 * DEPTH) ** 0.25

T_TOK = CTX_LEN + SEQ
ROWS = BATCH * T_TOK
LANES = 128
SUBLANES = 8
MLA_SLOT = 2 * LANES

P_AX, P_AG, P_BQ, P_BKV, P_BKR = 0, 1024, 2048, 2560, 2816
P_CQ, P_CK, P_CV, P_GATE = 3072, 4096, 4352, 4608
P_COLS = P_GATE + N_BRANCH * D_MODEL

VMEM_LIMIT = 56 * 1024 * 1024

TM_MIX = 1152
TN_MIX = 1536
TM_MERGE = 576
TJ_MERGE = 512
TM_FFN = 768
TJ_FFN = 512
TM_PREP = 576
TQ_ATT = 256
TQ_MLA = 384
TN_ADA = 2048
TM_MERGE_LATENT = 512
TM_FFN_LATENT = 512
LOG2E = math.log2(math.e)


def _params(sem):
    return pltpu.CompilerParams(dimension_semantics=sem, vmem_limit_bytes=VMEM_LIMIT)


def _row_select(i, tiles_per_batch, tm, ctx_vec, batch_vec):
    rows = lax.broadcasted_iota(jnp.int32, (tm, 1), 0)
    n_ctx = jnp.where(i % tiles_per_batch == 0, CTX_LEN, 0)
    return jnp.where(rows < n_ctx, ctx_vec, batch_vec)


def _layer_norm(v, g, b):
    mu = jnp.mean(v, -1, keepdims=True)
    d = v - mu
    var = jnp.mean(d * d, -1, keepdims=True)
    return d * lax.rsqrt(var + LN_EPS) * g + b


def _rms_norm(v, g):
    return v * lax.rsqrt(jnp.mean(v * v, -1, keepdims=True) + RMS_EPS) * g


def _swap_pairs(x):
    lane = lax.broadcasted_iota(jnp.int32, (1, LANES), 1)
    first = (lane & 16) == 0
    return jnp.where(first, pltpu.roll(x, LANES - 16, 1), pltpu.roll(x, 16, 1))


def _rope(x, cos, sin):
    return x * cos + _swap_pairs(x) * sin


def _ada_kernel(c_ref, w_ref, b_ref, o_ref, *, chunks_per_vec):
    j = pl.program_id(1)
    c = c_ref[...]
    a = (c * jax.nn.sigmoid(c)).astype(BF16)
    acc = jnp.dot(a, w_ref[0].astype(BF16), preferred_element_type=F32)
    vec = j // chunks_per_vec
    one = jnp.where((vec == 1) | (vec == 4), 1.0, 0.0)
    o_ref[0, 0] = acc + b_ref[0] + one


def _ada(cvec, w_ada, b_ada):
    tn = TN_ADA
    cpv = D_MODEL // tn
    return pl.pallas_call(
        functools.partial(_ada_kernel, chunks_per_vec=cpv),
        grid=(DEPTH, 6 * cpv),
        in_specs=[
            pl.BlockSpec((SUBLANES, D_MODEL), lambda l, j: (0, 0)),
            pl.BlockSpec((1, D_MODEL, tn), lambda l, j: (l, 0, j)),
            pl.BlockSpec((1, 1, tn), lambda l, j: (l, 0, j)),
        ],
        out_specs=pl.BlockSpec((1, 1, SUBLANES, tn), lambda l, j: (l, j // cpv, 0, j % cpv)),
        out_shape=jax.ShapeDtypeStruct((DEPTH, 6, SUBLANES, D_MODEL), F32),
        compiler_params=_params(("arbitrary", "arbitrary")),
        name="ada",
    )(cvec, w_ada, b_ada.reshape(DEPTH, 1, 6 * D_MODEL))


def _vec_specs(l, vec, tiles_per_batch):
    return [
        pl.BlockSpec((None, 1, 1, 1, D_MODEL), lambda i, j: (l, vec, i // tiles_per_batch, 0, 0)),
        pl.BlockSpec((None, 1, 1, 1, D_MODEL), lambda i, j: (l, vec, BATCH, 0, 0)),
    ]


def _layer_vec(l):
    return pl.BlockSpec((None, 1, D_MODEL), lambda i, j: (l, 0, 0))


def _row_spec(tm, width, col_block, latent_only):
    if not latent_only:
        return pl.BlockSpec((tm, width), lambda i, j: (i, col_block(j)))
    tiles = SEQ // tm

    def index(i, j):
        row = (i // tiles) * T_TOK + CTX_LEN + (i % tiles) * tm
        return pl.multiple_of(row, CTX_LEN), pl.multiple_of(col_block(j) * width, LANES)

    return pl.BlockSpec((pl.Element(tm), pl.Element(width)), index)


def _call_with_casts(body, *, grid, in_specs, out_spec, out_shape, args, casts, name, scratch_shapes=()):
    n_in, n_side = len(in_specs), len(casts)
    n_steps = math.prod(grid)
    bf16_rows = 2 * SUBLANES
    side_in, side_out, side_shape = [], [], []
    for src, layer in casts:
        _, n_rows, n_cols = src.shape
        n_blocks = max(b for b in range(1, n_steps + 1) if n_rows % (b * bf16_rows) == 0)
        rows = n_rows // n_blocks

        def block(*ids, n_blocks=n_blocks):
            step = ids[0] if len(ids) == 1 else ids[0] * grid[1] + ids[1]
            return jnp.minimum(step, n_blocks - 1)

        side_in.append(pl.BlockSpec((None, rows, n_cols), lambda *ids, block=block, layer=layer: (layer, block(*ids), 0)))
        side_out.append(pl.BlockSpec((rows, n_cols), lambda *ids, block=block: (block(*ids), 0)))
        side_shape.append(jax.ShapeDtypeStruct((n_rows, n_cols), BF16))

    def kernel(*refs):
        for src_ref, dst_ref in zip(refs[n_in:n_in + n_side], refs[n_in + n_side + 1:n_in + 2 * n_side + 1]):
            dst_ref[...] = src_ref[...].astype(dst_ref.dtype)
        body(*refs[:n_in], refs[n_in + n_side], *refs[n_in + 2 * n_side + 1:])

    out = pl.pallas_call(
        kernel,
        grid=grid,
        in_specs=list(in_specs) + side_in,
        out_specs=[out_spec] + side_out,
        out_shape=[out_shape] + side_shape,
        scratch_shapes=list(scratch_shapes),
        compiler_params=_params(("arbitrary",) * len(grid)),
        name=name,
    )(*args, *[c[0] for c in casts])
    return out[0], list(out[1:])


def _mixin_kernel(x_ref, scb_ref, scc_ref, shb_ref, shc_ref, w_ref, o_ref, u_ref, *, tpb, tm):
    i = pl.program_id(0)

    @pl.when(pl.program_id(1) == 0)
    def _():
        sc = _row_select(i, tpb, tm, scc_ref[0, 0], scb_ref[0, 0])
        sh = _row_select(i, tpb, tm, shc_ref[0, 0], shb_ref[0, 0])
        u_ref[...] = (x_ref[...] * sc + sh).astype(BF16)

    o_ref[...] = lax.dot_general(u_ref[...], w_ref[...], (((1,), (1,)), ((), ())),
                                 preferred_element_type=F32).astype(o_ref.dtype)


def _mixin(l, x, ada, w_t, casts):
    tm, tn = TM_MIX, TN_MIX
    tpb = T_TOK // tm
    pad = P_CQ - (P_BKR + MLA_ROPE)
    assert P_CQ % tn == 0 and w_t.shape[0] + pad == P_COLS

    def w_index(i, j):
        row = j * tn - jnp.where(j >= P_CQ // tn, pad, 0)
        return pl.multiple_of(row, MLA_ROPE), 0

    return _call_with_casts(
        functools.partial(_mixin_kernel, tpb=tpb, tm=tm),
        grid=(ROWS // tm, P_COLS // tn),
        in_specs=[pl.BlockSpec((tm, D_MODEL), lambda i, j: (i, 0))]
        + _vec_specs(l, 1, tpb)
        + _vec_specs(l, 0, tpb)
        + [pl.BlockSpec((pl.Element(tn), pl.Element(D_MODEL)), w_index)],
        out_spec=pl.BlockSpec((tm, tn), lambda i, j: (i, j)),
        out_shape=jax.ShapeDtypeStruct((ROWS, P_COLS), BF16),
        scratch_shapes=[pltpu.VMEM((tm, D_MODEL), BF16)],
        args=(x, ada, ada, ada, ada, w_t),
        casts=casts,
        name="mix_in",
    )


LRU_NB = 2
LRU_SEG_FWD = 292
LRU_SEG_BWD = 260
LRU_PAD_TOK = SUBLANES * LRU_SEG_FWD
assert LRU_PAD_TOK >= T_TOK and CTX_LEN + SUBLANES * LRU_SEG_BWD == LRU_PAD_TOK
assert LRU_SEG_FWD % 8 == 4 and LRU_SEG_BWD % 8 == 4


def _lru_kernel(ax_ref, ag_ref, cw_ref, cb_ref, wr_ref, br_ref, wi_ref, bi_ref, lam_ref, y_ref,
                a_s, b_s, h_s, p_s):
    n_tok = T_TOK
    nb = LRU_NB
    t = lax.broadcasted_iota(jnp.int32, (n_tok, 1), 0)
    is_ctx = t < CTX_LEN
    t_loc = jnp.where(is_ctx, t, t - CTX_LEN)
    seg_len = jnp.where(is_ctx, CTX_LEN, SEQ)

    for k in range(nb):
        lanes = slice(LANES * k, LANES * (k + 1))
        x = ax_ref[:, lanes].astype(F32)
        xm1 = jnp.where(t_loc >= 1, pltpu.roll(x, 1, 0), 0.0)
        xm2 = jnp.where(t_loc >= 2, pltpu.roll(x, 2, 0), 0.0)
        xp1 = jnp.where(t_loc <= seg_len - 2, pltpu.roll(x, n_tok - 1, 0), 0.0)
        w = cw_ref[:, lanes]
        xc = xm2 * w[0:1] + xm1 * w[1:2] + x * w[2:3] + xp1 * w[3:4] + cb_ref[:, lanes]
        xb = xc.astype(BF16)
        for d in range(2):
            i = d * nb + k
            tr = jnp.tanh(0.5 * (jnp.dot(xb, wr_ref[d, k], preferred_element_type=F32) + br_ref[d, :, lanes]))
            ti = jnp.tanh(0.5 * (jnp.dot(xb, wi_ref[d, k], preferred_element_type=F32) + bi_ref[d, :, lanes]))
            z = -lam_ref[d, :, lanes]
            softplus = jnp.maximum(z, 0.0) + jnp.log1p(jnp.exp(-jnp.abs(z)))
            half_c = (-0.5 * LRU_C) * softplus
            log_a = half_c * tr + half_c
            a = jnp.exp(log_a)
            v = jnp.tanh(-log_a) * (a * a + 1.0)
            mult = jnp.where(v > 0.0, v * lax.rsqrt(v), 0.0)
            a_s[i, 0:n_tok, :] = a
            b_s[i, 0:n_tok, :] = (0.5 * mult) * (ti + 1.0) * xc
            a_s[i, n_tok:LRU_PAD_TOK, :] = jnp.ones((LRU_PAD_TOK - n_tok, LANES), F32)
            b_s[i, n_tok:LRU_PAD_TOK, :] = jnp.zeros((LRU_PAD_TOK - n_tok, LANES), F32)

    sub = lax.broadcasted_iota(jnp.int32, (CTX_LEN, 1), 0) & (SUBLANES - 1)
    for k in range(nb):
        a = a_s[nb + k, 0:CTX_LEN, :]
        bb = b_s[nb + k, 0:CTX_LEN, :]
        for s in (1, 2, 4):
            m = sub < SUBLANES - s
            a_sh = pltpu.roll(a, CTX_LEN - s, 0)
            b_sh = pltpu.roll(bb, CTX_LEN - s, 0)
            bb = jnp.where(m, bb + a * b_sh, bb)
            a = jnp.where(m, a * a_sh, a)
        a_s[nb + k, 0:CTX_LEN, :] = a
        b_s[nb + k, 0:CTX_LEN, :] = bb

    def ctx_bwd(j, hs):
        rows = pl.ds(pl.multiple_of((CTX_LEN // SUBLANES - 1 - j) * SUBLANES, SUBLANES), SUBLANES)
        out = []
        for k in range(nb):
            hg = b_s[nb + k, rows, :] + a_s[nb + k, rows, :] * hs[k]
            b_s[nb + k, rows, :] = hg
            out.append(hg[0:1])
        return tuple(out)

    h_ctx0 = lax.fori_loop(0, CTX_LEN // SUBLANES, ctx_bwd, (jnp.zeros((1, LANES), F32),) * nb, unroll=4)

    def seg_rows(base, s, seg):
        return pl.ds(base + s, SUBLANES, stride=seg)

    def scan_step(i, rows, h, p):
        av = a_s[i, rows, :]
        h = av * h + b_s[i, rows, :]
        p = av * p
        h_s[i, rows, :] = h
        p_s[i, rows, :] = p
        return h, p

    def fwd_steps(s, c):
        rows = seg_rows(0, s, LRU_SEG_FWD)
        return sum((scan_step(k, rows, c[2 * k], c[2 * k + 1]) for k in range(nb)), ())

    def bwd_steps(s, c):
        rows = seg_rows(CTX_LEN, LRU_SEG_BWD - 1 - s, LRU_SEG_BWD)
        return sum((scan_step(nb + k, rows, c[2 * k], c[2 * k + 1]) for k in range(nb)), ())

    zeros8 = jnp.zeros((SUBLANES, LANES), F32)
    ones8 = jnp.ones((SUBLANES, LANES), F32)
    init = (zeros8, ones8) * nb

    def both(s, c):
        return fwd_steps(s, c[:2 * nb]) + bwd_steps(s, c[2 * nb:])

    c = lax.fori_loop(0, LRU_SEG_BWD, both, init + init, unroll=2)
    cf_state = lax.fori_loop(LRU_SEG_BWD, LRU_SEG_FWD, fwd_steps, c[:2 * nb], unroll=2)
    cb_state = c[2 * nb:]

    seg = lax.broadcasted_iota(jnp.int32, (SUBLANES, 1), 0)
    cfs, cbs = [], []
    for k in range(nb):
        hf, pf = cf_state[2 * k], cf_state[2 * k + 1]
        hb, pb = cb_state[2 * k], cb_state[2 * k + 1]
        cf = zeros8
        cb = jnp.where(seg == SUBLANES - 1, h_ctx0[k], 0.0)
        for _ in range(SUBLANES - 1):
            cf = jnp.where(seg >= 1, pltpu.roll(hf + pf * cf, 1, 0), 0.0)
            cb = jnp.where(seg <= SUBLANES - 2, pltpu.roll(hb + pb * cb, SUBLANES - 1, 0), h_ctx0[k])
        cfs.append(cf)
        cbs.append(cb)

    def fix_fwd(s):
        rows = seg_rows(0, s, LRU_SEG_FWD)
        for k in range(nb):
            b_s[k, rows, :] = h_s[k, rows, :] + p_s[k, rows, :] * cfs[k]

    def fix_bwd(s):
        rows = seg_rows(CTX_LEN, s, LRU_SEG_BWD)
        for k in range(nb):
            b_s[nb + k, rows, :] = h_s[nb + k, rows, :] + p_s[nb + k, rows, :] * cbs[k]

    def fix_both(s, c):
        fix_fwd(s)
        fix_bwd(s)
        return c

    def fix_tail(s, c):
        fix_fwd(s)
        return c

    lax.fori_loop(0, LRU_SEG_BWD, fix_both, 0, unroll=2)
    lax.fori_loop(LRU_SEG_BWD, LRU_SEG_FWD, fix_tail, 0, unroll=2)

    for k in range(nb):
        lanes = slice(LANES * k, LANES * (k + 1))
        g = ag_ref[:, lanes].astype(F32)
        cdf = 0.5 * (1.0 + jnp.tanh(math.sqrt(2.0 / math.pi) * (g + 0.044715 * (g * g * g))))
        h_sum = b_s[k, 0:n_tok, :] + b_s[nb + k, 0:n_tok, :]
        y_ref[:, lanes] = (h_sum * (g * cdf)).astype(y_ref.dtype)


def _lru(l, p, conv_w, conv_b, wr, br, wi, bi, lam, casts):
    assert LRU_BLOCK_W == LANES
    cw = LRU_NB * LANES
    col = lambda off: (lambda b, n: (b, off // cw + n))
    vec2 = pl.BlockSpec((None, 2, 1, cw), lambda b, n: (l, 0, 0, n))
    mat = pl.BlockSpec((None, 2, LRU_NB, LRU_BLOCK_W, LRU_BLOCK_W), lambda b, n: (l, 0, n, 0, 0))
    scan_buf = pltpu.VMEM((2 * LRU_NB, LRU_PAD_TOK, LANES), F32)
    return _call_with_casts(
        _lru_kernel,
        grid=(BATCH, LRU_WIDTH // cw),
        in_specs=[
            pl.BlockSpec((T_TOK, cw), col(P_AX)),
            pl.BlockSpec((T_TOK, cw), col(P_AG)),
            pl.BlockSpec((None, CONV_W, cw), lambda b, n: (l, 0, n)),
            pl.BlockSpec((None, 1, cw), lambda b, n: (l, 0, n)),
            mat, vec2, mat, vec2, vec2,
        ],
        out_spec=pl.BlockSpec((T_TOK, cw), lambda b, n: (b, n)),
        out_shape=jax.ShapeDtypeStruct((ROWS, LRU_WIDTH), BF16),
        scratch_shapes=[scan_buf, scan_buf, scan_buf, scan_buf],
        args=(p, p, conv_w, conv_b.reshape(DEPTH, 1, LRU_WIDTH), wr, br.reshape(DEPTH, 2, 1, LRU_WIDTH), wi,
              bi.reshape(DEPTH, 2, 1, LRU_WIDTH), lam.reshape(DEPTH, 2, 1, LRU_WIDTH)),
        casts=casts,
        name="rg_lru",
    )


def _mla_prep_kernel(bq_ref, bkv_ref, bkr_ref, qn_ref, wq_ref, kvn_ref, wk_ref, wv_ref, cos_ref, sin_ref,
                     q_out, k_out, v_out):
    cos = cos_ref[...]
    sin = sin_ref[...]
    qn = _rms_norm(bq_ref[...].astype(F32), qn_ref[...])
    q = jnp.dot(qn.astype(BF16), wq_ref[...], preferred_element_type=F32) * (MLA_SCALE * LOG2E)
    kvn = _rms_norm(bkv_ref[...].astype(F32), kvn_ref[...]).astype(BF16)
    kn = jnp.dot(kvn, wk_ref[...], preferred_element_type=F32).astype(BF16)
    v_out[...] = jnp.dot(kvn, wv_ref[...], preferred_element_type=F32).astype(BF16)
    lane = lax.broadcasted_iota(jnp.int32, (1, LANES), 1)
    kr = jnp.where(lane < MLA_ROPE, bkr_ref[...].astype(F32), 0.0)
    kr = _rope(kr, cos, sin).astype(BF16)
    for h in range(MLA_HEADS):
        lo = MLA_SLOT * h
        q_out[:, lo:lo + LANES] = q[:, lo:lo + LANES].astype(BF16)
        q_out[:, lo + LANES:lo + MLA_SLOT] = _rope(q[:, lo + LANES:lo + MLA_SLOT], cos, sin).astype(BF16)
        k_out[:, lo:lo + LANES] = kn[:, LANES * h:LANES * (h + 1)]
        k_out[:, lo + LANES:lo + MLA_SLOT] = kr


def _mla_prep(l, p, q_norm, wq_p, kv_norm, wk_p, wv_p, cos, sin):
    tm = TM_PREP
    tpb = T_TOK // tm
    full = lambda shape: pl.BlockSpec((None,) + shape, lambda i: (l,) + (0,) * len(shape))
    return pl.pallas_call(
        _mla_prep_kernel,
        grid=(ROWS // tm,),
        in_specs=[
            pl.BlockSpec((tm, MLA_Q_RANK), lambda i: (i, P_BQ // MLA_Q_RANK)),
            pl.BlockSpec((tm, MLA_KV_RANK), lambda i: (i, P_BKV // MLA_KV_RANK)),
            pl.BlockSpec((tm, LANES), lambda i: (i, P_BKR // LANES)),
            full((1, MLA_Q_RANK)),
            full((MLA_Q_RANK, MLA_HEADS * MLA_SLOT)),
            full((1, MLA_KV_RANK)),
            full((MLA_KV_RANK, MLA_HEADS * MLA_NOPE)),
            full((MLA_KV_RANK, MLA_HEADS * MLA_V)),
            pl.BlockSpec((tm, LANES), lambda i: (i % tpb, 0)),
            pl.BlockSpec((tm, LANES), lambda i: (i % tpb, 0)),
        ],
        out_specs=[
            pl.BlockSpec((tm, MLA_HEADS * MLA_SLOT), lambda i: (i, 0)),
            pl.BlockSpec((tm, MLA_HEADS * MLA_SLOT), lambda i: (i, 0)),
            pl.BlockSpec((tm, MLA_HEADS * MLA_V), lambda i: (i, 0)),
        ],
        out_shape=[
            jax.ShapeDtypeStruct((ROWS, MLA_HEADS * MLA_SLOT), BF16),
            jax.ShapeDtypeStruct((ROWS, MLA_HEADS * MLA_SLOT), BF16),
            jax.ShapeDtypeStruct((ROWS, MLA_HEADS * MLA_V), BF16),
        ],
        compiler_params=_params(("arbitrary",)),
        name="mla_prep",
    )(p, p, p, q_norm.reshape(DEPTH, 1, -1), wq_p, kv_norm.reshape(DEPTH, 1, -1), wk_p, wv_p, cos, sin)


def _mla_attn_kernel(q_ref, k_ref, v_ref, o_ref):
    tq = q_ref.shape[0]

    def attend(row0, n_rows, n_keys):
        for h in range(MLA_HEADS):
            q = q_ref[row0:row0 + n_rows, MLA_SLOT * h:MLA_SLOT * (h + 1)]
            k = k_ref[0:n_keys, MLA_SLOT * h:MLA_SLOT * (h + 1)]
            s = lax.dot_general(q, k, (((1,), (1,)), ((), ())), preferred_element_type=F32)
            m = jnp.max(s, -1, keepdims=True)
            p = jnp.exp2(s - m)
            l = jnp.sum(p, -1, keepdims=True)
            o = jnp.dot(p.astype(BF16), v_ref[0:n_keys, MLA_V * h:MLA_V * (h + 1)], preferred_element_type=F32)
            o_ref[row0:row0 + n_rows, MLA_V * h:MLA_V * (h + 1)] = (o / l).astype(o_ref.dtype)

    @pl.when(pl.program_id(1) == 0)
    def _():
        attend(0, CTX_LEN, CTX_LEN)
        attend(CTX_LEN, tq - CTX_LEN, T_TOK)

    @pl.when(pl.program_id(1) > 0)
    def _():
        attend(0, tq, T_TOK)


def _mla_attn(q, k, v):
    tq = TQ_MLA
    assert tq > CTX_LEN and T_TOK % tq == 0 and tq % (2 * SUBLANES) == 0
    tpb = T_TOK // tq
    per_batch = lambda width: pl.BlockSpec((T_TOK, width), lambda b, i: (b, 0))
    return pl.pallas_call(
        _mla_attn_kernel,
        grid=(BATCH, tpb),
        in_specs=[
            pl.BlockSpec((tq, MLA_HEADS * MLA_SLOT), lambda b, i: (b * tpb + i, 0)),
            per_batch(MLA_HEADS * MLA_SLOT),
            per_batch(MLA_HEADS * MLA_V),
        ],
        out_specs=pl.BlockSpec((tq, MLA_HEADS * MLA_V), lambda b, i: (b * tpb + i, 0)),
        out_shape=jax.ShapeDtypeStruct((ROWS, MLA_HEADS * MLA_V), BF16),
        compiler_params=_params(("arbitrary", "arbitrary")),
        name="mla_attn",
    )(q, k, v)


def _dup_half(x, upper):
    lane = lax.broadcasted_iota(jnp.int32, (1, LANES), 1)
    r = pltpu.roll(x, LANES // 2, 1)
    out = jnp.where(lane < LANES // 2, r, x) if upper else jnp.where(lane < LANES // 2, x, r)
    return out.astype(BF16)


def _swa_kernel(q_ref, k_ref, v_ref, cos_ref, sin_ref, sink_ref, o_ref, *, layer):
    tq = TQ_ATT
    band = tq + 2 * WINDOW
    qt = pl.program_id(1)
    q_row = pl.multiple_of(qt * tq, tq)
    lane = lax.broadcasted_iota(jnp.int32, (1, LANES), 1)
    low = lane < LANES // 2
    cos_q = cos_ref[pl.ds(q_row, tq), :]
    sin_q = sin_ref[pl.ds(q_row, tq), :]

    def q_block(c):
        x = q_ref[:, LANES * c:LANES * (c + 1)].astype(F32)
        return _rope(x, cos_q, sin_q) * (SWA_SCALE * LOG2E)

    def softmax_out(qm, sink, keys, vals, masks):
        ss = []
        for kk, mk in zip(keys, masks):
            s = lax.dot_general(qm, kk, (((1,), (1,)), ((), ())), preferred_element_type=F32)
            ss.append(s if mk is None else jnp.where(mk, s, -jnp.inf))
        m = jnp.maximum(functools.reduce(jnp.maximum, [jnp.max(s, -1, keepdims=True) for s in ss]), sink)
        ps = [jnp.exp2(s - m) for s in ss]
        l = functools.reduce(jnp.add, [jnp.sum(p, -1, keepdims=True) for p in ps]) + jnp.exp2(sink - m)
        o = functools.reduce(jnp.add, [jnp.dot(p.astype(BF16), vv, preferred_element_type=F32)
                                       for p, vv in zip(ps, vals)])
        return o / l

    def run(keys_of, vals_of, masks):
        for g in range(SWA_KV_HEADS):
            keys = keys_of(g)
            vals = vals_of(g)
            qms, sinks = [], []
            for c in (2 * g, 2 * g + 1):
                qc = q_block(c)
                qms += [jnp.where(low, qc, 0.0).astype(BF16), jnp.where(low, 0.0, qc).astype(BF16)]
                sinks += [sink_ref[layer, 2 * c] * LOG2E, sink_ref[layer, 2 * c + 1] * LOG2E]
            scores = [[lax.dot_general(qm, kk, (((1,), (1,)), ((), ())), preferred_element_type=F32) for qm in qms]
                      for kk in keys]
            ps, ls = [], []
            for h in range(SWA_REP):
                ss = [s[h] if mk is None else jnp.where(mk, s[h], -jnp.inf) for s, mk in zip(scores, masks)]
                m = jnp.maximum(functools.reduce(jnp.maximum, [jnp.max(s, -1, keepdims=True) for s in ss]), sinks[h])
                pr = [jnp.exp2(s - m) for s in ss]
                ls.append(functools.reduce(jnp.add, [jnp.sum(p, -1, keepdims=True) for p in pr]) + jnp.exp2(sinks[h] - m))
                ps.append([p.astype(BF16) for p in pr])
            outs = [[jnp.dot(ps[h][b], vv, preferred_element_type=F32) for h in range(SWA_REP)]
                    for b, vv in enumerate(vals)]
            os_ = [functools.reduce(jnp.add, [outs[b][h] for b in range(len(vals))]) / ls[h] for h in range(SWA_REP)]
            for i, c in enumerate((2 * g, 2 * g + 1)):
                o_ref[:, LANES * c:LANES * (c + 1)] = jnp.where(low, os_[2 * i], os_[2 * i + 1]).astype(o_ref.dtype)

    def ctx_kv(ref, g):
        return _dup_half(ref[0:CTX_LEN, LANES * (g // 2):LANES * (g // 2 + 1)].astype(F32), g % 2 == 1)

    @pl.when(qt == 0)
    def _():
        if layer == DEPTH - 1:
            o_ref[...] = jnp.zeros_like(o_ref)
        else:
            run(lambda g: [ctx_kv(k_ref, g)], lambda g: [ctx_kv(v_ref, g)], [None])

    @pl.when(qt > 0)
    def _():
        start = pl.multiple_of(jnp.minimum(q_row - WINDOW, T_TOK - band), LANES)
        qpos = q_row - CTX_LEN + lax.broadcasted_iota(jnp.int32, (tq, 1), 0)
        kpos = start - CTX_LEN + lax.broadcasted_iota(jnp.int32, (1, band), 1)
        valid = (jnp.abs(kpos - qpos) <= WINDOW) & (kpos >= 0)
        cos_k = cos_ref[pl.ds(start, band), :]
        sin_k = sin_ref[pl.ds(start, band), :]

        def band_k(g):
            kb = k_ref[pl.ds(start, band), LANES * (g // 2):LANES * (g // 2 + 1)].astype(F32)
            return _dup_half(_rope(kb, cos_k, sin_k), g % 2 == 1)

        def band_v(g):
            return _dup_half(v_ref[pl.ds(start, band), LANES * (g // 2):LANES * (g // 2 + 1)].astype(F32), g % 2 == 1)

        run(lambda g: [ctx_kv(k_ref, g), band_k(g)], lambda g: [ctx_kv(v_ref, g), band_v(g)], [None, valid])


def _swa(l, p, cos, sin, sinks, casts):
    tq = TQ_ATT
    tpb = T_TOK // tq
    kvw = SWA_KV_HEADS * SWA_HEAD_DIM
    return _call_with_casts(
        functools.partial(_swa_kernel, layer=l),
        grid=(BATCH, tpb),
        in_specs=[
            pl.BlockSpec((tq, SWA_HEADS * SWA_HEAD_DIM), lambda b, i: (b * tpb + i, P_CQ // (SWA_HEADS * SWA_HEAD_DIM))),
            pl.BlockSpec((T_TOK, kvw), lambda b, i: (b, P_CK // kvw)),
            pl.BlockSpec((T_TOK, kvw), lambda b, i: (b, P_CV // kvw)),
            pl.BlockSpec((T_TOK, LANES), lambda b, i: (0, 0)),
            pl.BlockSpec((T_TOK, LANES), lambda b, i: (0, 0)),
            pl.BlockSpec(memory_space=pltpu.SMEM),
        ],
        out_spec=pl.BlockSpec((tq, SWA_HEADS * SWA_HEAD_DIM), lambda b, i: (b * tpb + i, 0)),
        out_shape=jax.ShapeDtypeStruct((ROWS, SWA_HEADS * SWA_HEAD_DIM), BF16),
        args=(p, p, p, cos, sin, sinks),
        casts=casts,
        name="swa",
    )


def _gate_rows(i, tpb, tm, gvc_ref, gvb_ref, has_ctx):
    return _row_select(i, tpb, tm, gvc_ref[0, 0], gvb_ref[0, 0]) if has_ctx else gvb_ref[0, 0]


def _merge_kernel(ya_ref, yb_ref, yc_ref, ga_ref, gb_ref, gc_ref, wb_ref, wo_ref, x_ref, gvb_ref, gvc_ref,
                  lng_ref, lnb_ref, o_ref, *, tpb, tm, nj, has_ctx):
    i = pl.program_id(0)
    j = pl.program_id(1)

    @pl.when(j == 0)
    def _():
        o_ref[...] = jnp.zeros_like(o_ref)

    z = None
    for n, (y_ref, g_ref) in enumerate(((ya_ref, ga_ref), (yb_ref, gb_ref), (yc_ref, gc_ref))):
        zn = jnp.dot(y_ref[...], wb_ref[n], preferred_element_type=F32)
        zn = jax.nn.sigmoid(g_ref[...].astype(F32)) * zn
        z = zn if z is None else z + zn
    o_ref[...] += jnp.dot(z.astype(BF16), wo_ref[...], preferred_element_type=F32)

    @pl.when(j == nj - 1)
    def _():
        gate = _gate_rows(i, tpb, tm, gvc_ref, gvb_ref, has_ctx)
        o_ref[...] = _layer_norm(ALPHA * x_ref[...] + gate * o_ref[...], lng_ref[...], lnb_ref[...])


def _merge(l, ya, yb, yc, p, wb, wo, x, ada, ln_g, ln_b, latent_only, casts):
    tm = TM_MERGE_LATENT if latent_only else TM_MERGE
    tj = TJ_MERGE
    tpb = (SEQ if latent_only else T_TOK) // tm
    n_rows = BATCH * SEQ if latent_only else ROWS
    nj = D_MODEL // tj
    ybs = _row_spec(tm, BRANCH_W, lambda j: 0, latent_only)
    gate = lambda n: _row_spec(tm, tj, lambda j: (P_GATE + n * D_MODEL) // tj + j, latent_only)
    return _call_with_casts(
        functools.partial(_merge_kernel, tpb=tpb, tm=tm, nj=nj, has_ctx=not latent_only),
        grid=(n_rows // tm, nj),
        in_specs=[ybs, ybs, ybs, gate(0), gate(1), gate(2),
                  pl.BlockSpec((N_BRANCH, BRANCH_W, tj), lambda i, j: (0, 0, j)),
                  pl.BlockSpec((tj, D_MODEL), lambda i, j: (j, 0)),
                  _row_spec(tm, D_MODEL, lambda j: 0, latent_only)]
        + _vec_specs(l, 2, tpb) + [_layer_vec(l), _layer_vec(l)],
        out_spec=pl.BlockSpec((tm, D_MODEL), lambda i, j: (i, 0)),
        out_shape=jax.ShapeDtypeStruct((n_rows, D_MODEL), F32),
        args=(ya, yb, yc, p, p, p, wb, wo, x, ada, ada, ln_g.reshape(DEPTH, 1, -1), ln_b.reshape(DEPTH, 1, -1)),
        casts=casts,
        name="merge",
    )


def _ffn_kernel(x_ref, scb_ref, scc_ref, shb_ref, shc_ref, gvb_ref, gvc_ref, wg_ref, wu_ref, wo_ref,
                lng_ref, lnb_ref, o_ref, u_ref, *, tpb, tm, nj, has_ctx):
    i = pl.program_id(0)
    j = pl.program_id(1)

    @pl.when(j == 0)
    def _():
        sc = _gate_rows(i, tpb, tm, scc_ref, scb_ref, has_ctx)
        sh = _gate_rows(i, tpb, tm, shc_ref, shb_ref, has_ctx)
        u_ref[...] = (x_ref[...] * sc + sh).astype(BF16)
        o_ref[...] = jnp.zeros_like(o_ref)

    u = u_ref[...]
    gt = jnp.dot(u, wg_ref[...], preferred_element_type=F32)
    up = jnp.dot(u, wu_ref[...], preferred_element_type=F32)
    h = (gt * jax.nn.sigmoid(gt) * up).astype(BF16)
    o_ref[...] += jnp.dot(h, wo_ref[...], preferred_element_type=F32)

    @pl.when(j == nj - 1)
    def _():
        gate = _gate_rows(i, tpb, tm, gvc_ref, gvb_ref, has_ctx)
        o_ref[...] = _layer_norm(ALPHA * x_ref[...] + gate * o_ref[...], lng_ref[...], lnb_ref[...])


def _ffn(l, x, ada, w_in, w_out, ln_g, ln_b, latent_only, casts):
    tm = TM_FFN_LATENT if latent_only else TM_FFN
    tj = TJ_FFN
    tpb = (SEQ if latent_only else T_TOK) // tm
    n_rows = x.shape[0]
    nj = FFN_HIDDEN // tj
    row = pl.BlockSpec((tm, D_MODEL), lambda i, j: (i, 0))
    row_in = row
    return _call_with_casts(
        functools.partial(_ffn_kernel, tpb=tpb, tm=tm, nj=nj, has_ctx=not latent_only),
        grid=(n_rows // tm, nj),
        in_specs=[row_in] + _vec_specs(l, 4, tpb) + _vec_specs(l, 3, tpb) + _vec_specs(l, 5, tpb) + [
            pl.BlockSpec((D_MODEL, tj), lambda i, j: (0, j)),
            pl.BlockSpec((D_MODEL, tj), lambda i, j: (0, nj + j)),
            pl.BlockSpec((tj, D_MODEL), lambda i, j: (j, 0)),
            _layer_vec(l), _layer_vec(l)],
        out_spec=row,
        out_shape=jax.ShapeDtypeStruct((n_rows, D_MODEL), F32),
        scratch_shapes=[pltpu.VMEM((tm, D_MODEL), BF16)],
        args=(x, ada, ada, ada, ada, ada, ada, w_in, w_in, w_out, ln_g.reshape(DEPTH, 1, -1), ln_b.reshape(DEPTH, 1, -1)),
        casts=casts,
        name="ffn",
    )


def _rope_tables(both_halves):
    rot = SWA_HEAD_DIM
    half = rot // 2
    n_rows = SEQ // GRID_W
    row = jnp.repeat(jnp.arange(n_rows, dtype=F32), GRID_W)
    colp = jnp.tile(jnp.arange(GRID_W, dtype=F32), n_rows)
    inv = ROPE_BASE ** (-jnp.arange(0, half, 2, dtype=F32) / half)
    ang_r = row[:, None] * inv
    ang_c = colp[:, None] * inv
    cos = jnp.concatenate([jnp.cos(ang_r), jnp.cos(ang_r), jnp.cos(ang_c), jnp.cos(ang_c)], -1)
    sin = jnp.concatenate([-jnp.sin(ang_r), jnp.sin(ang_r), -jnp.sin(ang_c), jnp.sin(ang_c)], -1)
    if both_halves:
        cos = jnp.concatenate([cos, cos], -1)
        sin = jnp.concatenate([sin, sin], -1)
    else:
        cos = jnp.concatenate([cos, jnp.ones_like(cos)], -1)
        sin = jnp.concatenate([sin, jnp.zeros_like(sin)], -1)
    cos = jnp.concatenate([jnp.ones((CTX_LEN, LANES), F32), cos], 0)
    sin = jnp.concatenate([jnp.zeros((CTX_LEN, LANES), F32), sin], 0)
    return cos, sin


def _mla_weights(w_q_up, w_kv_up):
    wq = w_q_up.reshape(DEPTH, MLA_Q_RANK, MLA_HEADS, MLA_NOPE + MLA_ROPE)
    wq = jnp.pad(wq, ((0, 0), (0, 0), (0, 0), (0, MLA_SLOT - MLA_NOPE - MLA_ROPE)))
    wkv = w_kv_up.reshape(DEPTH, MLA_KV_RANK, MLA_HEADS, MLA_NOPE + MLA_V)
    wk = wkv[..., :MLA_NOPE].reshape(DEPTH, MLA_KV_RANK, MLA_HEADS * MLA_NOPE)
    wv = wkv[..., MLA_NOPE:].reshape(DEPTH, MLA_KV_RANK, MLA_HEADS * MLA_V)
    return wq.reshape(DEPTH, MLA_Q_RANK, MLA_HEADS * MLA_SLOT).astype(BF16), wk.astype(BF16), wv.astype(BF16)


def kernel(x, c, ctx, c_ctx, w_ada, b_ada, w_in, conv_w, conv_b, lru_wr, lru_br, lru_wi, lru_bi, lru_lambda, mla_q_norm, mla_w_q_up, mla_kv_norm, mla_w_kv_up, swa_sinks, w_branch, w_out, ln1_g, ln1_b, w_ffn_in, w_ffn_out, ln2_g, ln2_b):
    cos_mla, sin_mla = _rope_tables(False)
    cos_swa, sin_swa = _rope_tables(True)
    cvec = jnp.concatenate([c, c_ctx[None], jnp.zeros((SUBLANES - BATCH - 1, D_MODEL), F32)], 0)
    ada = _ada(cvec, w_ada, b_ada).reshape(DEPTH, 6, SUBLANES, 1, D_MODEL)
    w_in_t = jnp.transpose(w_in, (0, 2, 1))
    w_t = w_in_t[0].astype(BF16)
    w_branch_rows = w_branch.reshape(DEPTH, N_BRANCH * BRANCH_W, D_MODEL)
    wr, wi = lru_wr.astype(BF16), lru_wi.astype(BF16)
    wq_p, wk_p, wv_p = _mla_weights(mla_w_q_up, mla_w_kv_up)
    xs = jnp.concatenate([ctx, x], 1).reshape(ROWS, D_MODEL)
    for l in range(DEPTH):
        last = l == DEPTH - 1
        p, (wf_out,) = _mixin(l, xs, ada, w_t, [(w_ffn_out, l)])
        ya, (wf_in,) = _lru(l, p, conv_w, conv_b, wr, lru_br, wi, lru_bi, lru_lambda, [(w_ffn_in, l)])
        q, k, v = _mla_prep(l, p, mla_q_norm, wq_p, mla_kv_norm, wk_p, wv_p, cos_mla, sin_mla)
        yb = _mla_attn(q, k, v)
        yc, (wb, wo) = _swa(l, p, cos_swa, sin_swa, swa_sinks, [(w_branch_rows, l), (w_out, l)])
        x1, _ = _merge(l, ya, yb, yc, p, wb.reshape(N_BRANCH, BRANCH_W, D_MODEL), wo, xs, ada, ln1_g, ln1_b, last, [])
        xs, next_w = _ffn(l, x1, ada, wf_in, wf_out, ln2_g, ln2_b, last, [] if last else [(w_in_t, l + 1)])
        if not last:
            w_t = next_w[0]
    return xs.reshape(BATCH, SEQ, D_MODEL)
```
